```python
import math
import jax, jax.numpy as jnp
from jax import lax
import numpy as np

D_MODEL = 1024
BATCH = 8
SEQ = 8192
DEPTH = 4

N_MIXERS = 2
BRANCH = D_MODEL
SSM_GROUP = 16
SSM_GROUPS = BRANCH // SSM_GROUP
SSM_STATE = 64
SSM_CHUNK = 128
DT_MIN = 1e-3
DT_MAX = 1e-1
HEAD_DIM = 64
N_Q_HEADS = BRANCH // HEAD_DIM
N_KV_HEADS = 2
GQA_GROUP = N_Q_HEADS // N_KV_HEADS
WINDOW = 128
ATTN_BLOCK = 128
Q_DIM = N_Q_HEADS * HEAD_DIM
KV_DIM = N_KV_HEADS * HEAD_DIM
ROPE_THETA = 10000.0
NORM_EPS = 1e-5
NEG_INF = -1e30

kernel_name = "hybrid_s5_swa_sink_trunk"


def _rmsnorm(x, g):
    xf = x.astype(jnp.float32)
    y = xf * lax.rsqrt(jnp.mean(xf * xf, axis=-1, keepdims=True) + NORM_EPS)
    return (y * g.astype(jnp.float32)).astype(x.dtype)


def _ssm_combine(e1, e2):
    a1, b1 = e1
    a2, b2 = e2
    return a1 * a2, a2 * b1 + b2


def _s5_scan(u, a_re, a_im, log_step, b_re, b_im, c_re, c_im, d):
    bsz, seq, _ = u.shape
    f32 = jnp.float32
    u = u.astype(f32)
    lam = lax.complex(a_re.astype(f32), a_im.astype(f32))
    step = jnp.exp(log_step.astype(f32))[:, None]
    a_bar = jnp.exp(lam * step)
    b = lax.complex(b_re.astype(f32), b_im.astype(f32))
    b_bar = ((a_bar - 1.0) / lam)[..., None] * b
    c = lax.complex(c_re.astype(f32), c_im.astype(f32))
    n_chunks = seq // SSM_CHUNK
    u_c = u.reshape(bsz, n_chunks, SSM_CHUNK, SSM_GROUPS, SSM_GROUP).transpose(1, 2, 0, 3, 4)

    def step_fn(h_prev, u_blk):
        bu = jnp.einsum("tbgc,gpc->tbgp", u_blk.astype(b_bar.dtype), b_bar)
        a = jnp.broadcast_to(a_bar, bu.shape)
        a_cum, h_loc = lax.associative_scan(_ssm_combine, (a, bu), axis=0)
        h = h_loc + a_cum * h_prev[None]
        y = jnp.real(jnp.einsum("tbgp,gcp->tbgc", h, c))
        return h[-1], y

    h0 = jnp.zeros((bsz, SSM_GROUPS, SSM_STATE), dtype=b_bar.dtype)
    _, y = lax.scan(step_fn, h0, u_c)
    y = y.transpose(2, 0, 1, 3, 4).reshape(bsz, seq, BRANCH)
    return y + d.astype(f32) * u


def _ssm_layer(x, norm, w_in, a_re, a_im, log_step, b_re, b_im, c_re, c_im, d, w_glu, b_glu, w_out):
    f32 = jnp.float32
    h = _rmsnorm(x, norm)
    proj = h @ w_in
    u, gate = jnp.split(proj, [BRANCH], axis=-1)
    y = _s5_scan(u, a_re, a_im, log_step, b_re, b_im, c_re, c_im, d)
    z = jax.nn.gelu(y)
    z = z * jax.nn.sigmoid(z @ w_glu.astype(f32) + b_glu.astype(f32))
    out = (z * jax.nn.silu(gate.astype(f32))).astype(x.dtype) @ w_out
    return x + out


def _rope(t, cos, sin):
    t1, t2 = jnp.split(t, 2, axis=-1)
    return jnp.concatenate([t1 * cos - t2 * sin, t2 * cos + t1 * sin], axis=-1)


def _swa_sinks(q, k, v, sinks):
    f32 = jnp.float32
    bsz, seq = q.shape[:2]
    nb = seq // ATTN_BLOCK
    qb = q.reshape(bsz, nb, ATTN_BLOCK, N_KV_HEADS, GQA_GROUP, HEAD_DIM)
    kb = k.reshape(bsz, nb, ATTN_BLOCK, N_KV_HEADS, HEAD_DIM)
    vb = v.reshape(bsz, nb, ATTN_BLOCK, N_KV_HEADS, HEAD_DIM)

    def with_prev(t):
        prev = jnp.concatenate([jnp.zeros_like(t[:, :1]), t[:, :-1]], axis=1)
        return jnp.concatenate([prev, t], axis=2)

    kk = with_prev(kb)
    vv = with_prev(vb)
    s = jnp.einsum("bnqhgd,bnkhd->bnhgqk", qb, kk) * (HEAD_DIM ** -0.5)
    qi = jnp.arange(ATTN_BLOCK)[:, None]
    kj = jnp.arange(2 * ATTN_BLOCK)[None, :]
    dist = qi + ATTN_BLOCK - kj
    band = (dist >= 0) & (dist < WINDOW)
    blk = jnp.arange(nb)[:, None, None]
    valid = band[None] & ((blk > 0) | (kj[None] >= ATTN_BLOCK))
    s = jnp.where(valid[None, :, None, None], s, NEG_INF)
    sink = sinks.astype(f32).reshape(N_KV_HEADS, GQA_GROUP)[None, None, :, :, None, None]
    m = jnp.maximum(jnp.max(s, axis=-1, keepdims=True), sink)
    p = jnp.exp(s - m)
    denom = jnp.sum(p, axis=-1, keepdims=True) + jnp.exp(sink - m)
    o = jnp.einsum("bnhgqk,bnkhd->bnqhgd", p / denom, vv)
    return o.reshape(bsz, seq, Q_DIM)


def _attn_layer(x, norm, w_in, sinks, w_out):
    f32 = jnp.float32
    bsz, seq, _ = x.shape
    h = _rmsnorm(x, norm)
    proj = (h @ w_in).astype(f32)
    q, k, v, gate = jnp.split(proj, [Q_DIM, Q_DIM + KV_DIM, Q_DIM + 2 * KV_DIM], axis=-1)
    q = q.reshape(bsz, seq, N_Q_HEADS, HEAD_DIM)
    k = k.reshape(bsz, seq, N_KV_HEADS, HEAD_DIM)
    v = v.reshape(bsz, seq, N_KV_HEADS, HEAD_DIM)
    pos = jnp.arange(seq, dtype=f32)
    inv_freq = ROPE_THETA ** (-jnp.arange(0, HEAD_DIM, 2, dtype=f32) / HEAD_DIM)
    ang = pos[:, None] * inv_freq[None, :]
    cos = jnp.cos(ang)[None, :, None, :]
    sin = jnp.sin(ang)[None, :, None, :]
    o = _swa_sinks(_rope(q, cos, sin), _rope(k, cos, sin), v, sinks)
    out = (o * jax.nn.silu(gate)).astype(x.dtype) @ w_out
    return x + out


def setup_inputs(seed: int = 0) -> dict:
    key = jax.random.key(seed)
    keys = iter(jax.random.split(key, 64))
    f32 = jnp.float32

    def nrm(shape, scale):
        return jax.random.normal(next(keys), shape, f32) * scale

    inputs = {"x": nrm((BATCH, SEQ, D_MODEL), 1.0)}
    for i in range(DEPTH):
        p = "l%d_" % i
        inputs[p + "norm"] = 1.0 + nrm((D_MODEL,), 0.05)
        if i % N_MIXERS == 0:
            inputs[p + "w_in"] = nrm((D_MODEL, 2 * BRANCH), D_MODEL ** -0.5)
            inputs[p + "a_re"] = -0.5 + nrm((SSM_GROUPS, SSM_STATE), 0.01)
            inputs[p + "a_im"] = math.pi * jnp.arange(SSM_STATE, dtype=f32)[None, :] + nrm((SSM_GROUPS, SSM_STATE), 0.01)
            inputs[p + "log_step"] = jax.random.uniform(next(keys), (SSM_GROUPS,), f32, math.log(DT_MIN), math.log(DT_MAX))
            inputs[p + "b_re"] = nrm((SSM_GROUPS, SSM_STATE, SSM_GROUP), (2 * SSM_GROUP) ** -0.5)
            inputs[p + "b_im"] = nrm((SSM_GROUPS, SSM_STATE, SSM_GROUP), (2 * SSM_GROUP) ** -0.5)
            inputs[p + "c_re"] = nrm((SSM_GROUPS, SSM_GROUP, SSM_STATE), SSM_STATE ** -0.5)
            inputs[p + "c_im"] = nrm((SSM_GROUPS, SSM_GROUP, SSM_STATE), SSM_STATE ** -0.5)
            inputs[p + "d"] = nrm((BRANCH,), 1.0)
            inputs[p + "w_glu"] = nrm((BRANCH, BRANCH), BRANCH ** -0.5)
            inputs[p + "b_glu"] = nrm((BRANCH,), 0.01)
            inputs[p + "w_out"] = nrm((BRANCH, D_MODEL), BRANCH ** -0.5)
        else:
            inputs[p + "w_in"] = nrm((D_MODEL, Q_DIM + 2 * KV_DIM + BRANCH), D_MODEL ** -0.5)
            inputs[p + "sinks"] = nrm((N_Q_HEADS,), 1.0)
            inputs[p + "w_out"] = nrm((Q_DIM, D_MODEL), Q_DIM ** -0.5)
    inputs["final_norm"] = 1.0 + nrm((D_MODEL,), 0.05)
    return inputs


def reference(x,
              l0_norm, l0_w_in, l0_a_re, l0_a_im, l0_log_step, l0_b_re, l0_b_im, l0_c_re, l0_c_im, l0_d, l0_w_glu, l0_b_glu, l0_w_out,
              l1_norm, l1_w_in, l1_sinks, l1_w_out,
              l2_norm, l2_w_in, l2_a_re, l2_a_im, l2_log_step, l2_b_re, l2_b_im, l2_c_re, l2_c_im, l2_d, l2_w_glu, l2_b_glu, l2_w_out,
              l3_norm, l3_w_in, l3_sinks, l3_w_out,
              final_norm):
    ssm_params = [
        (l0_norm, l0_w_in, l0_a_re, l0_a_im, l0_log_step, l0_b_re, l0_b_im, l0_c_re, l0_c_im, l0_d, l0_w_glu, l0_b_glu, l0_w_out),
        (l2_norm, l2_w_in, l2_a_re, l2_a_im, l2_log_step, l2_b_re, l2_b_im, l2_c_re, l2_c_im, l2_d, l2_w_glu, l2_b_glu, l2_w_out),
    ]
    attn_params = [
        (l1_norm, l1_w_in, l1_sinks, l1_w_out),
        (l3_norm, l3_w_in, l3_sinks, l3_w_out),
    ]
    for i in range(DEPTH):
        if i % N_MIXERS == 0:
            x = _ssm_layer(x, *ssm_params[i // N_MIXERS])
        else:
            x = _attn_layer(x, *attn_params[i // N_MIXERS])
    return _rmsnorm(x, final_norm)
```

```python
import functools
import math

import numpy as np
import jax
import jax.numpy as jnp
from jax import lax
from jax.experimental import pallas as pl
from jax.experimental.pallas import tpu as pltpu

D_MODEL = 1024
BATCH = 8
BRANCH = D_MODEL
SSM_GROUP = 16
SSM_GROUPS = BRANCH // SSM_GROUP
SSM_STATE = 64
HEAD_DIM = 64
N_Q_HEADS = BRANCH // HEAD_DIM
N_KV_HEADS = 2
ATTN_BLOCK = 128
ROPE_THETA = 10000.0
NORM_EPS = 1e-5
NEG_INF = -1e30

LANES = 128
MXU_TILE = 256
GROUPS_PER_TILE = MXU_TILE // SSM_GROUP
STATE_COLS = GROUPS_PER_TILE * SSM_STATE
N_CH_TILES = BRANCH // MXU_TILE
HEADS_PER_TILE = MXU_TILE // HEAD_DIM
HALF = HEAD_DIM // 2

SSM_T_BLOCK = 64
ATTN_ROWS = 512
VMEM_LIMIT = 56 * 1024 * 1024

F32 = jnp.float32
BF16 = jnp.bfloat16


def _sigmoid(v):
    return 1.0 / (1.0 + jnp.exp(-v))


def _gelu_tanh(v):
    c = math.sqrt(2.0 / math.pi)
    return 0.5 * v * (1.0 + jnp.tanh(c * (v + 0.044715 * (v * v * v))))


def _rms_scale(x, g):
    ms = jnp.mean(x * x, axis=-1, keepdims=True)
    return x * lax.rsqrt(ms + NORM_EPS) * g


def _const_spec(shape):
    nd = len(shape)
    return pl.BlockSpec(shape, lambda *_: (0,) * nd, pipeline_mode=pl.Buffered(1))


def _ssm_prep_kernel(are_ref, aim_ref, ls_ref, bre_ref, bim_ref, abr_ref, abi_ref, bbr_ref, bbi_ref):
    lr = are_ref[...]
    li = aim_ref[...]
    step = jnp.exp(ls_ref[...])
    mag = jnp.exp(lr * step)
    ar = mag * jnp.cos(li * step)
    ai = mag * jnp.sin(li * step)
    xr = ar - 1.0
    den = lr * lr + li * li
    cr = (xr * lr + ai * li) / den
    ci = (ai * lr - xr * li) / den
    br = bre_ref[...]
    bi = bim_ref[...]
    abr_ref[...] = ar
    abi_ref[...] = ai
    bbr_ref[...] = cr * br - ci * bi
    bbi_ref[...] = cr * bi + ci * br


def _ssm_prep(a_re, a_im, log_step, b_re, b_im):
    g, p, c = b_re.shape
    rep = lambda a: jnp.repeat(a.astype(F32), c, axis=1)
    ls = jnp.broadcast_to(log_step.astype(F32)[:, None], (g, p * c))
    shp = jax.ShapeDtypeStruct((g, p * c), F32)
    abr, abi, bbr, bbi = pl.pallas_call(
        _ssm_prep_kernel, out_shape=(shp, shp, shp, shp), name="ssm_prep",
    )(rep(a_re), rep(a_im), ls, b_re.astype(F32).reshape(g, p * c), b_im.astype(F32).reshape(g, p * c))
    a_bar_re = abr.reshape(g, p, c)[:, :, 0]
    a_bar_im = abi.reshape(g, p, c)[:, :, 0]
    return a_bar_re, a_bar_im, bbr.reshape(g, p, c), bbi.reshape(g, p, c)


def _ssm_matrices(a_bar_re, a_bar_im, bb_re, bb_im, c_re, c_im):
    eye = jnp.eye(GROUPS_PER_TILE, dtype=F32)
    bb = jnp.stack([bb_re, bb_im]).reshape(2, N_CH_TILES, GROUPS_PER_TILE, SSM_STATE, SSM_GROUP)
    bm = jnp.einsum("ab,rjbpc->jacrbp", eye, bb).reshape(N_CH_TILES, MXU_TILE, 2 * STATE_COLS)
    cc = jnp.stack([c_re.astype(F32), -c_im.astype(F32)]).reshape(2, N_CH_TILES, GROUPS_PER_TILE, SSM_GROUP, SSM_STATE)
    cm = jnp.einsum("ab,rjacp->jrapbc", eye, cc).reshape(N_CH_TILES, 2 * STATE_COLS, MXU_TILE)
    ar8 = jnp.broadcast_to(a_bar_re.reshape(1, -1), (BATCH, SSM_GROUPS * SSM_STATE))
    ai8 = jnp.broadcast_to(a_bar_im.reshape(1, -1), (BATCH, SSM_GROUPS * SSM_STATE))
    return bm.astype(BF16), cm.astype(BF16), ar8, ai8


def _ssm_layer_kernel(x_ref, nrm_ref, win_ref, bm_ref, cm_ref, ar_ref, ai_ref, d_ref, wglu_ref, bglu_ref,
                      wout_ref, o_ref, h_scr, proj_scr, bu_scr, y_scr, *, t_block):
    @pl.when(pl.program_id(0) == 0)
    def _():
        h_scr[...] = jnp.zeros_like(h_scr)

    x = x_ref[...]
    hn = _rms_scale(x, nrm_ref[...]).astype(BF16)
    proj_scr[...] = jnp.dot(hn, win_ref[...], preferred_element_type=F32)

    scan_w = STATE_COLS // 2
    for j in range(N_CH_TILES):
        ch = slice(j * MXU_TILE, (j + 1) * MXU_TILE)
        u_j = proj_scr[:, ch]
        bu_scr[...] = jnp.dot(u_j.astype(BF16), bm_ref[j], preferred_element_type=F32)
        for k in range(STATE_COLS // scan_w):
            re_cols = slice(k * scan_w, (k + 1) * scan_w)
            im_cols = slice(STATE_COLS + k * scan_w, STATE_COLS + (k + 1) * scan_w)
            a_cols = slice(j * STATE_COLS + k * scan_w, j * STATE_COLS + (k + 1) * scan_w)
            h_re = slice(j * 2 * STATE_COLS + k * scan_w, j * 2 * STATE_COLS + (k + 1) * scan_w)
            h_im = slice(j * 2 * STATE_COLS + STATE_COLS + k * scan_w, j * 2 * STATE_COLS + STATE_COLS + (k + 1) * scan_w)
            ar = ar_ref[:, a_cols]
            ai = ai_ref[:, a_cols]

            def step(t, carry, re_cols=re_cols, im_cols=im_cols, ar=ar, ai=ai):
                hr, hi = carry
                rows = pl.ds(pl.multiple_of(t * BATCH, BATCH), BATCH)
                nr = ar * hr - ai * hi + bu_scr[rows, re_cols]
                ni = ai * hr + ar * hi + bu_scr[rows, im_cols]
                bu_scr[rows, re_cols] = nr
                bu_scr[rows, im_cols] = ni
                return nr, ni

            hr, hi = lax.fori_loop(0, t_block, step, (h_scr[:, h_re], h_scr[:, h_im]), unroll=2)
            h_scr[:, h_re] = hr
            h_scr[:, h_im] = hi
        y_j = jnp.dot(bu_scr[...].astype(BF16), cm_ref[j], preferred_element_type=F32)
        y_scr[:, ch] = y_j + d_ref[:, ch] * u_j

    z = _gelu_tanh(y_scr[...])
    glu = jnp.dot(z.astype(BF16), wglu_ref[...], preferred_element_type=F32) + bglu_ref[...]
    z = z * _sigmoid(glu)
    gate = proj_scr[:, BRANCH:]
    act = (z * (gate * _sigmoid(gate))).astype(BF16)
    o_ref[...] = x + jnp.dot(act, wout_ref[...], preferred_element_type=F32)


def _ssm_layer(x_tb, norm, w_in, a_re, a_im, log_step, b_re, b_im, c_re, c_im, d, w_glu, b_glu, w_out):
    rows, dm = x_tb.shape
    seq = rows // BATCH
    t_block = min(SSM_T_BLOCK, seq)
    tm = t_block * BATCH
    a_bar_re, a_bar_im, bb_re, bb_im = _ssm_prep(a_re, a_im, log_step, b_re, b_im)
    bm, cm, ar8, ai8 = _ssm_matrices(a_bar_re, a_bar_im, bb_re, bb_im, c_re, c_im)
    n_state = SSM_GROUPS * SSM_STATE
    row_spec = pl.BlockSpec((tm, dm), lambda i: (i, 0))
    return pl.pallas_call(
        functools.partial(_ssm_layer_kernel, t_block=t_block),
        out_shape=jax.ShapeDtypeStruct((rows, dm), F32),
        grid=(seq // t_block,),
        in_specs=[
            row_spec,
            _const_spec((1, dm)),
            _const_spec((dm, 2 * BRANCH)),
            _const_spec(bm.shape),
            _const_spec(cm.shape),
            _const_spec((BATCH, n_state)),
            _const_spec((BATCH, n_state)),
            _const_spec((1, BRANCH)),
            _const_spec((BRANCH, BRANCH)),
            _const_spec((1, BRANCH)),
            _const_spec((BRANCH, dm)),
        ],
        out_specs=row_spec,
        scratch_shapes=[
            pltpu.VMEM((BATCH, 2 * n_state), F32),
            pltpu.VMEM((tm, 2 * BRANCH), F32),
            pltpu.VMEM((tm, 2 * STATE_COLS), F32),
            pltpu.VMEM((tm, BRANCH), F32),
        ],
        compiler_params=pltpu.CompilerParams(dimension_semantics=("arbitrary",), vmem_limit_bytes=VMEM_LIMIT),
        name="ssm_layer",
    )(x_tb, norm.astype(F32).reshape(1, dm), w_in.astype(BF16), bm, cm, ar8, ai8,
      d.astype(F32).reshape(1, BRANCH), w_glu.astype(BF16), b_glu.astype(F32).reshape(1, BRANCH), w_out.astype(BF16))


def _rope_kernel(freq_ref, cos_ref, sin_ref):
    rows = cos_ref.shape[0]
    pos = (lax.broadcasted_iota(jnp.int32, cos_ref.shape, 0) + pl.program_id(0) * rows).astype(F32)
    ang = pos * freq_ref[...]
    cos_ref[...] = jnp.cos(ang)
    sin_ref[...] = jnp.sin(ang)


def _rope_tables(seq):
    inv_freq = (np.float32(ROPE_THETA) ** (-np.arange(0, HEAD_DIM, 2, dtype=np.float32) / np.float32(HEAD_DIM))).astype(np.float32)
    freq = jnp.asarray(np.tile(inv_freq, LANES // HALF).reshape(1, LANES))
    blk = min(1024, seq)
    spec = pl.BlockSpec((blk, LANES), lambda i: (i, 0))
    shp = jax.ShapeDtypeStruct((seq, LANES), F32)
    return pl.pallas_call(
        _rope_kernel, out_shape=(shp, shp), grid=(seq // blk,),
        in_specs=[pl.BlockSpec((1, LANES), lambda i: (0, 0))], out_specs=(spec, spec), name="rope_tables",
    )(freq)


def _attn_weight_layout(w_in):
    wq = w_in[:, :BRANCH] * (HEAD_DIM ** -0.5)
    wk = w_in[:, BRANCH:BRANCH + N_KV_HEADS * HEAD_DIM]
    wv = w_in[:, BRANCH + N_KV_HEADS * HEAD_DIM:BRANCH + 2 * N_KV_HEADS * HEAD_DIM]
    wg = w_in[:, BRANCH + 2 * N_KV_HEADS * HEAD_DIM:]
    dm = w_in.shape[0]
    wq = wq.reshape(dm, N_Q_HEADS // HEADS_PER_TILE, HEADS_PER_TILE, 2, HALF).transpose(0, 1, 3, 2, 4).reshape(dm, BRANCH)
    wk = jnp.broadcast_to(wk.reshape(dm, N_KV_HEADS, 2, 1, HALF), (dm, N_KV_HEADS, 2, HEADS_PER_TILE, HALF))
    wk = wk.reshape(dm, N_KV_HEADS * MXU_TILE)
    wv = jnp.broadcast_to(wv.reshape(dm, N_KV_HEADS, 1, HEAD_DIM), (dm, N_KV_HEADS, HEADS_PER_TILE, HEAD_DIM))
    wv = wv.reshape(dm, N_KV_HEADS * MXU_TILE)
    return jnp.concatenate([wq, wk, wv, wg], axis=1).astype(BF16)


_Q0 = 0
_K0 = BRANCH
_V0 = BRANCH + N_KV_HEADS * MXU_TILE
_G0 = BRANCH + 2 * N_KV_HEADS * MXU_TILE
_PROJ_COLS = _G0 + BRANCH


def _attn_layer_kernel(sinks_ref, x_ref, nrm_ref, win_ref, cos_ref, sin_ref, wout_ref, fin_ref, o_ref,
                       proj_scr, q_scr, kz_scr, vz_scr, o_scr, *, rows, final_norm):
    ti = pl.program_id(1)
    n_blk = rows // ATTN_BLOCK
    tiles_per_kv = N_Q_HEADS // HEADS_PER_TILE // N_KV_HEADS

    @pl.when(ti == 0)
    def _():
        kz_scr[:, :, :ATTN_BLOCK, :] = jnp.zeros((N_KV_HEADS, HEADS_PER_TILE, ATTN_BLOCK, MXU_TILE), BF16)
        vz_scr[:, :, :ATTN_BLOCK, :] = jnp.zeros((N_KV_HEADS, HEADS_PER_TILE, ATTN_BLOCK, MXU_TILE), BF16)

    @pl.when(ti > 0)
    def _():
        kz_scr[:, :, :ATTN_BLOCK, :] = kz_scr[:, :, rows:, :]
        vz_scr[:, :, :ATTN_BLOCK, :] = vz_scr[:, :, rows:, :]

    x = x_ref[...]
    hn = _rms_scale(x, nrm_ref[...]).astype(BF16)
    proj_scr[...] = jnp.dot(hn, win_ref[...], preferred_element_type=F32)

    cos = cos_ref[...]
    sin = sin_ref[...]

    def rope(t):
        t1 = t[:, :LANES]
        t2 = t[:, LANES:]
        return jnp.concatenate([t1 * cos - t2 * sin, t2 * cos + t1 * sin], axis=1)

    for m in range(N_Q_HEADS // HEADS_PER_TILE):
        q_scr[:, m * MXU_TILE:(m + 1) * MXU_TILE] = rope(proj_scr[:, _Q0 + m * MXU_TILE:_Q0 + (m + 1) * MXU_TILE]).astype(BF16)

    lane = lax.broadcasted_iota(jnp.int32, (1, MXU_TILE), 1)
    for kv in range(N_KV_HEADS):
        k_rep = rope(proj_scr[:, _K0 + kv * MXU_TILE:_K0 + (kv + 1) * MXU_TILE])
        v_rep = proj_scr[:, _V0 + kv * MXU_TILE:_V0 + (kv + 1) * MXU_TILE]
        for j in range(HEADS_PER_TILE):
            k_mask = (lane % LANES) // HALF == j
            v_mask = lane // HEAD_DIM == j
            kz_scr[kv, j, ATTN_BLOCK:, :] = jnp.where(k_mask, k_rep, 0.0).astype(BF16)
            vz_scr[kv, j, ATTN_BLOCK:, :] = jnp.where(v_mask, v_rep, 0.0).astype(BF16)

    qi = lax.broadcasted_iota(jnp.int32, (ATTN_BLOCK, 2 * ATTN_BLOCK), 0)
    kj = lax.broadcasted_iota(jnp.int32, (ATTN_BLOCK, 2 * ATTN_BLOCK), 1)
    dist = qi + ATTN_BLOCK - kj
    band = (dist >= 0) & (dist < ATTN_BLOCK)
    band_first = band & ((kj >= ATTN_BLOCK) | (ti > 0))

    for m in range(N_Q_HEADS // HEADS_PER_TILE):
        kv = m // tiles_per_kv
        for i in range(n_blk):
            q_rows = slice(i * ATTN_BLOCK, (i + 1) * ATTN_BLOCK)
            k_rows = slice(i * ATTN_BLOCK, (i + 2) * ATTN_BLOCK)
            valid = band_first if i == 0 else band
            q_blk = q_scr[q_rows, m * MXU_TILE:(m + 1) * MXU_TILE]
            acc = None
            for j in range(HEADS_PER_TILE):
                sink = sinks_ref[m * HEADS_PER_TILE + j]
                s = lax.dot_general(q_blk, kz_scr[kv, j, k_rows, :], (((1,), (1,)), ((), ())),
                                    preferred_element_type=F32)
                s = jnp.where(valid, s, NEG_INF)
                mx = jnp.maximum(jnp.max(s, axis=-1, keepdims=True), sink)
                p = jnp.exp(s - mx)
                denom = jnp.sum(p, axis=-1, keepdims=True) + jnp.exp(sink - mx)
                p = (p / denom).astype(BF16)
                pv = jnp.dot(p, vz_scr[kv, j, k_rows, :], preferred_element_type=F32)
                acc = pv if acc is None else acc + pv
            o_scr[q_rows, m * MXU_TILE:(m + 1) * MXU_TILE] = acc

    gate = proj_scr[:, _G0:]
    act = (o_scr[...] * (gate * _sigmoid(gate))).astype(BF16)
    y = x + jnp.dot(act, wout_ref[...], preferred_element_type=F32)
    if final_norm:
        y = _rms_scale(y, fin_ref[...])
    o_ref[...] = y


def _attn_layer(x_bt, norm, w_in, sinks, w_out, cos, sin, fin, *, final_norm):
    bsz, seq, dm = x_bt.shape
    rows = min(ATTN_ROWS, seq)
    win = _attn_weight_layout(w_in.astype(F32))
    row_spec = pl.BlockSpec((None, rows, dm), lambda b, t: (b, t, 0))
    tab_spec = pl.BlockSpec((rows, LANES), lambda b, t: (t, 0))
    return pl.pallas_call(
        functools.partial(_attn_layer_kernel, rows=rows, final_norm=final_norm),
        out_shape=jax.ShapeDtypeStruct((bsz, seq, dm), F32),
        grid=(bsz, seq // rows),
        in_specs=[
            pl.BlockSpec(memory_space=pltpu.SMEM),
            row_spec,
            _const_spec((1, dm)),
            _const_spec((dm, _PROJ_COLS)),
            tab_spec,
            tab_spec,
            _const_spec((BRANCH, dm)),
            _const_spec((1, dm)),
        ],
        out_specs=row_spec,
        scratch_shapes=[
            pltpu.VMEM((rows, _PROJ_COLS), F32),
            pltpu.VMEM((rows, BRANCH), BF16),
            pltpu.VMEM((N_KV_HEADS, HEADS_PER_TILE, ATTN_BLOCK + rows, MXU_TILE), BF16),
            pltpu.VMEM((N_KV_HEADS, HEADS_PER_TILE, ATTN_BLOCK + rows, MXU_TILE), BF16),
            pltpu.VMEM((rows, BRANCH), F32),
        ],
        compiler_params=pltpu.CompilerParams(dimension_semantics=("arbitrary", "arbitrary"), vmem_limit_bytes=VMEM_LIMIT),
        name="attn_layer",
    )(sinks.astype(F32), x_bt, norm.astype(F32).reshape(1, dm), win, cos, sin, w_out.astype(BF16),
      fin.astype(F32).reshape(1, dm))


def _to_time_major(x_bt):
    bsz, seq, dm = x_bt.shape
    return jnp.transpose(x_bt, (1, 0, 2)).reshape(seq * bsz, dm)


def _to_batch_major(x_tb, bsz):
    rows, dm = x_tb.shape
    return jnp.transpose(x_tb.reshape(rows // bsz, bsz, dm), (1, 0, 2))


def kernel(x, l0_norm, l0_w_in, l0_a_re, l0_a_im, l0_log_step, l0_b_re, l0_b_im, l0_c_re, l0_c_im, l0_d, l0_w_glu, l0_b_glu, l0_w_out, l1_norm, l1_w_in, l1_sinks, l1_w_out, l2_norm, l2_w_in, l2_a_re, l2_a_im, l2_log_step, l2_b_re, l2_b_im, l2_c_re, l2_c_im, l2_d, l2_w_glu, l2_b_glu, l2_w_out, l3_norm, l3_w_in, l3_sinks, l3_w_out, final_norm):
    bsz, seq, _ = x.shape
    assert bsz == BATCH
    cos, sin = _rope_tables(seq)
    h = x.astype(F32)
    h = _ssm_layer(_to_time_major(h), l0_norm, l0_w_in, l0_a_re, l0_a_im, l0_log_step, l0_b_re, l0_b_im,
                   l0_c_re, l0_c_im, l0_d, l0_w_glu, l0_b_glu, l0_w_out)
    h = _attn_layer(_to_batch_major(h, bsz), l1_norm, l1_w_in, l1_sinks, l1_w_out, cos, sin, final_norm,
                    final_norm=False)
    h = _ssm_layer(_to_time_major(h), l2_norm, l2_w_in, l2_a_re, l2_a_im, l2_log_step, l2_b_re, l2_b_im,
                   l2_c_re, l2_c_im, l2_d, l2_w_glu, l2_b_glu, l2_w_out)
    h = _attn_layer(_to_batch_major(h, bsz), l3_norm, l3_w_in, l3_sinks, l3_w_out, cos, sin, final_norm,
                    final_norm=True)
    return h.astype(x.dtype)
```

```python
import functools
import math

import numpy as np
import jax
import jax.numpy as jnp
from jax import lax
from jax.experimental import pallas as pl
from jax.experimental.pallas import tpu as pltpu

D_MODEL = 1024
BATCH = 8
BRANCH = D_MODEL
SSM_GROUP = 16
SSM_GROUPS = BRANCH // SSM_GROUP
SSM_STATE = 64
HEAD_DIM = 64
N_Q_HEADS = BRANCH // HEAD_DIM
N_KV_HEADS = 2
ATTN_BLOCK = 128
ROPE_THETA = 10000.0
NORM_EPS = 1e-5
NEG_INF = -1e30

LANES = 128
MXU_TILE = 256
GROUPS_PER_TILE = MXU_TILE // SSM_GROUP
STATE_COLS = GROUPS_PER_TILE * SSM_STATE
N_CH_TILES = BRANCH // MXU_TILE
HEADS_PER_TILE = MXU_TILE // HEAD_DIM
HALF = HEAD_DIM // 2

SSM_T_BLOCK = 64
ATTN_ROWS = 512
VMEM_LIMIT = 56 * 1024 * 1024

F32 = jnp.float32
BF16 = jnp.bfloat16


def _sigmoid(v):
    return 1.0 / (1.0 + jnp.exp(-v))


def _gelu_tanh(v):
    c = math.sqrt(2.0 / math.pi)
    return 0.5 * v * (1.0 + jnp.tanh(c * (v + 0.044715 * (v * v * v))))


def _rms_scale(x, g):
    ms = jnp.mean(x * x, axis=-1, keepdims=True)
    return x * lax.rsqrt(ms + NORM_EPS) * g


def _const_spec(shape):
    nd = len(shape)
    return pl.BlockSpec(shape, lambda *_: (0,) * nd, pipeline_mode=pl.Buffered(1))


def _ssm_prep_kernel(are_ref, aim_ref, ls_ref, bre_ref, bim_ref, abr_ref, abi_ref, bbr_ref, bbi_ref):
    lr = are_ref[...]
    li = aim_ref[...]
    step = jnp.exp(ls_ref[...])
    mag = jnp.exp(lr * step)
    ar = mag * jnp.cos(li * step)
    ai = mag * jnp.sin(li * step)
    xr = ar - 1.0
    den = lr * lr + li * li
    cr = (xr * lr + ai * li) / den
    ci = (ai * lr - xr * li) / den
    br = bre_ref[...]
    bi = bim_ref[...]
    abr_ref[...] = ar
    abi_ref[...] = ai
    bbr_ref[...] = cr * br - ci * bi
    bbi_ref[...] = cr * bi + ci * br


def _ssm_prep(a_re, a_im, log_step, b_re, b_im):
    g, p, c = b_re.shape
    rep = lambda a: jnp.repeat(a.astype(F32), c, axis=1)
    ls = jnp.broadcast_to(log_step.astype(F32)[:, None], (g, p * c))
    shp = jax.ShapeDtypeStruct((g, p * c), F32)
    abr, abi, bbr, bbi = pl.pallas_call(
        _ssm_prep_kernel, out_shape=(shp, shp, shp, shp), name="ssm_prep",
    )(rep(a_re), rep(a_im), ls, b_re.astype(F32).reshape(g, p * c), b_im.astype(F32).reshape(g, p * c))
    a_bar_re = abr.reshape(g, p, c)[:, :, 0]
    a_bar_im = abi.reshape(g, p, c)[:, :, 0]
    return a_bar_re, a_bar_im, bbr.reshape(g, p, c), bbi.reshape(g, p, c)


def _ssm_matrices(a_bar_re, a_bar_im, bb_re, bb_im, c_re, c_im):
    eye = jnp.eye(GROUPS_PER_TILE, dtype=F32)
    bb = jnp.stack([bb_re, bb_im]).reshape(2, N_CH_TILES, GROUPS_PER_TILE, SSM_STATE, SSM_GROUP)
    bm = jnp.einsum("ab,rjbpc->jacrbp", eye, bb).reshape(N_CH_TILES, MXU_TILE, 2 * STATE_COLS)
    cc = jnp.stack([c_re.astype(F32), -c_im.astype(F32)]).reshape(2, N_CH_TILES, GROUPS_PER_TILE, SSM_GROUP, SSM_STATE)
    cm = jnp.einsum("ab,rjacp->jrapbc", eye, cc).reshape(N_CH_TILES, 2 * STATE_COLS, MXU_TILE)
    ar8 = jnp.broadcast_to(a_bar_re.reshape(1, -1), (BATCH, SSM_GROUPS * SSM_STATE))
    ai8 = jnp.broadcast_to(a_bar_im.reshape(1, -1), (BATCH, SSM_GROUPS * SSM_STATE))
    return bm.astype(BF16), cm.astype(BF16), ar8, ai8


def _ssm_layer_kernel(x_ref, nrm_ref, win_ref, bm_ref, cm_ref, ar_ref, ai_ref, d_ref, wglu_ref, bglu_ref,
                      wout_ref, o_ref, h_scr, proj_scr, bu_scr, y_scr, *, t_block):
    @pl.when(pl.program_id(0) == 0)
    def _():
        h_scr[...] = jnp.zeros_like(h_scr)

    x = x_ref[...]
    hn = _rms_scale(x, nrm_ref[...]).astype(BF16)
    proj_scr[...] = jnp.dot(hn, win_ref[...], preferred_element_type=F32)

    scan_w = STATE_COLS // 2
    for j in range(N_CH_TILES):
        ch = slice(j * MXU_TILE, (j + 1) * MXU_TILE)
        u_j = proj_scr[:, ch]
        bu_scr[...] = jnp.dot(u_j.astype(BF16), bm_ref[j], preferred_element_type=F32)
        for k in range(STATE_COLS // scan_w):
            re_cols = slice(k * scan_w, (k + 1) * scan_w)
            im_cols = slice(STATE_COLS + k * scan_w, STATE_COLS + (k + 1) * scan_w)
            a_cols = slice(j * STATE_COLS + k * scan_w, j * STATE_COLS + (k + 1) * scan_w)
            h_re = slice(j * 2 * STATE_COLS + k * scan_w, j * 2 * STATE_COLS + (k + 1) * scan_w)
            h_im = slice(j * 2 * STATE_COLS + STATE_COLS + k * scan_w, j * 2 * STATE_COLS + STATE_COLS + (k + 1) * scan_w)
            ar = ar_ref[:, a_cols]
            ai = ai_ref[:, a_cols]

            def step(t, carry, re_cols=re_cols, im_cols=im_cols, ar=ar, ai=ai):
                hr, hi = carry
                rows = pl.ds(pl.multiple_of(t * BATCH, BATCH), BATCH)
                nr = ar * hr - ai * hi + bu_scr[rows, re_cols]
                ni = ai * hr + ar * hi + bu_scr[rows, im_cols]
                bu_scr[rows, re_cols] = nr
                bu_scr[rows, im_cols] = ni
                return nr, ni

            hr, hi = lax.fori_loop(0, t_block, step, (h_scr[:, h_re], h_scr[:, h_im]), unroll=2)
            h_scr[:, h_re] = hr
            h_scr[:, h_im] = hi
        y_j = jnp.dot(bu_scr[...].astype(BF16), cm_ref[j], preferred_element_type=F32)
        y_scr[:, ch] = y_j + d_ref[:, ch] * u_j

    z = _gelu_tanh(y_scr[...])
    glu = jnp.dot(z.astype(BF16), wglu_ref[...], preferred_element_type=F32) + bglu_ref[...]
    z = z * _sigmoid(glu)
    gate = proj_scr[:, BRANCH:]
    act = (z * (gate * _sigmoid(gate))).astype(BF16)
    o_ref[...] = x + jnp.dot(act, wout_ref[...], preferred_element_type=F32)


def _ssm_layer(x_tb, norm, w_in, a_re, a_im, log_step, b_re, b_im, c_re, c_im, d, w_glu, b_glu, w_out):
    rows, dm = x_tb.shape
    seq = rows // BATCH
    t_block = min(SSM_T_BLOCK, seq)
    tm = t_block * BATCH
    a_bar_re, a_bar_im, bb_re, bb_im = _ssm_prep(a_re, a_im, log_step, b_re, b_im)
    bm, cm, ar8, ai8 = _ssm_matrices(a_bar_re, a_bar_im, bb_re, bb_im, c_re, c_im)
    n_state = SSM_GROUPS * SSM_STATE
    row_spec = pl.BlockSpec((tm, dm), lambda i: (i, 0))
    return pl.pallas_call(
        functools.partial(_ssm_layer_kernel, t_block=t_block),
        out_shape=jax.ShapeDtypeStruct((rows, dm), F32),
        grid=(seq // t_block,),
        in_specs=[
            row_spec,
            _const_spec((1, dm)),
            _const_spec((dm, 2 * BRANCH)),
            _const_spec(bm.shape),
            _const_spec(cm.shape),
            _const_spec((BATCH, n_state)),
            _const_spec((BATCH, n_state)),
            _const_spec((1, BRANCH)),
            _const_spec((BRANCH, BRANCH)),
            _const_spec((1, BRANCH)),
            _const_spec((BRANCH, dm)),
        ],
        out_specs=row_spec,
        scratch_shapes=[
            pltpu.VMEM((BATCH, 2 * n_state), F32),
            pltpu.VMEM((tm, 2 * BRANCH), F32),
            pltpu.VMEM((tm, 2 * STATE_COLS), F32),
            pltpu.VMEM((tm, BRANCH), F32),
        ],
        compiler_params=pltpu.CompilerParams(dimension_semantics=("arbitrary",), vmem_limit_bytes=VMEM_LIMIT),
        name="ssm_layer",
    )(x_tb, norm.astype(F32).reshape(1, dm), w_in.astype(BF16), bm, cm, ar8, ai8,
      d.astype(F32).reshape(1, BRANCH), w_glu.astype(BF16), b_glu.astype(F32).reshape(1, BRANCH), w_out.astype(BF16))


def _rope_kernel(freq_ref, cos_ref, sin_ref):
    rows = cos_ref.shape[0]
    pos = (lax.broadcasted_iota(jnp.int32, cos_ref.shape, 0) + pl.program_id(0) * rows).astype(F32)
    ang = pos * freq_ref[...]
    lane = lax.broadcasted_iota(jnp.int32, cos_ref.shape, 1)
    sin = jnp.sin(ang)
    cos_ref[...] = jnp.cos(ang)
    sin_ref[...] = jnp.where(lane < LANES // 2, -sin, sin)


def _rope_tables(seq):
    inv_freq = (np.float32(ROPE_THETA) ** (-np.arange(0, HEAD_DIM, 2, dtype=np.float32) / np.float32(HEAD_DIM))).astype(np.float32)
    freq = jnp.asarray(np.tile(inv_freq, LANES // HALF).reshape(1, LANES))
    blk = min(1024, seq)
    spec = pl.BlockSpec((blk, LANES), lambda i: (i, 0))
    shp = jax.ShapeDtypeStruct((seq, LANES), F32)
    return pl.pallas_call(
        _rope_kernel, out_shape=(shp, shp), grid=(seq // blk,),
        in_specs=[pl.BlockSpec((1, LANES), lambda i: (0, 0))], out_specs=(spec, spec), name="rope_tables",
    )(freq)


def _attn_weight_layout(w_in):
    wq = w_in[:, :BRANCH] * (HEAD_DIM ** -0.5)
    wk = w_in[:, BRANCH:BRANCH + N_KV_HEADS * HEAD_DIM]
    wv = w_in[:, BRANCH + N_KV_HEADS * HEAD_DIM:BRANCH + 2 * N_KV_HEADS * HEAD_DIM]
    wg = w_in[:, BRANCH + 2 * N_KV_HEADS * HEAD_DIM:]
    dm = w_in.shape[0]
    wq = wq.reshape(dm, N_Q_HEADS // 2, 2, 2, HALF).transpose(0, 1, 3, 2, 4).reshape(dm, BRANCH)
    wk = jnp.broadcast_to(wk.reshape(dm, N_KV_HEADS, 2, 1, HALF), (dm, N_KV_HEADS, 2, 2, HALF))
    wk = wk.reshape(dm, N_KV_HEADS * LANES)
    wv = jnp.broadcast_to(wv.reshape(dm, N_KV_HEADS, 1, HEAD_DIM), (dm, N_KV_HEADS, 2, HEAD_DIM))
    wv = wv.reshape(dm, N_KV_HEADS * LANES)
    return jnp.concatenate([wq, wk, wv, wg], axis=1).astype(BF16)


_Q0 = 0
_K0 = BRANCH
_V0 = BRANCH + N_KV_HEADS * LANES
_G0 = BRANCH + 2 * N_KV_HEADS * LANES
_PROJ_COLS = _G0 + BRANCH
PAIRS_PER_KV = N_Q_HEADS // 2 // N_KV_HEADS


def _attn_layer_kernel(sinks_ref, x_ref, nrm_ref, win_ref, cos_ref, sin_ref, wout_ref, fin_ref, o_ref,
                       proj_scr, q_scr, kz_scr, vz_scr, o_scr, bias_scr, s_scr, mx_scr, es_scr, *, rows, final_norm):
    ti = pl.program_id(1)
    n_blk = rows // ATTN_BLOCK
    n_units = N_KV_HEADS * n_blk
    stack = PAIRS_PER_KV * ATTN_BLOCK
    window = 2 * ATTN_BLOCK

    @pl.when((pl.program_id(0) == 0) & (ti == 0))
    def _():
        qi = lax.broadcasted_iota(jnp.int32, (stack, window), 0) % ATTN_BLOCK
        kj = lax.broadcasted_iota(jnp.int32, (stack, window), 1)
        dist = qi + ATTN_BLOCK - kj
        band = (dist >= 0) & (dist < ATTN_BLOCK)
        bias_scr[0] = jnp.where(band, 0.0, NEG_INF)
        bias_scr[1] = jnp.where(band & (kj >= ATTN_BLOCK), 0.0, NEG_INF)

    @pl.when(ti == 0)
    def _():
        kz_scr[:, :, :ATTN_BLOCK, :] = jnp.zeros((N_KV_HEADS, 2, ATTN_BLOCK, LANES), BF16)
        vz_scr[:, :, :ATTN_BLOCK, :] = jnp.zeros((N_KV_HEADS, 2, ATTN_BLOCK, LANES), BF16)

    @pl.when(ti > 0)
    def _():
        kz_scr[:, :, :ATTN_BLOCK, :] = kz_scr[:, :, rows:, :]
        vz_scr[:, :, :ATTN_BLOCK, :] = vz_scr[:, :, rows:, :]

    x = x_ref[...]
    hn = _rms_scale(x, nrm_ref[...]).astype(BF16)
    proj_scr[...] = jnp.dot(hn, win_ref[...], preferred_element_type=F32)

    cos = cos_ref[...]
    sin = sin_ref[...]

    def rope(t):
        return t * cos + pltpu.roll(t, LANES // 2, axis=1) * sin

    for r in range(N_Q_HEADS // 2):
        q_scr[r] = rope(proj_scr[:, _Q0 + r * LANES:_Q0 + (r + 1) * LANES]).astype(BF16)

    lane = lax.broadcasted_iota(jnp.int32, (1, LANES), 1)
    for kv in range(N_KV_HEADS):
        k_rep = rope(proj_scr[:, _K0 + kv * LANES:_K0 + (kv + 1) * LANES])
        v_rep = proj_scr[:, _V0 + kv * LANES:_V0 + (kv + 1) * LANES]
        for slot in range(2):
            k_mask = (lane // HALF) % 2 == slot
            v_mask = lane // HEAD_DIM == slot
            kz_scr[kv, slot, ATTN_BLOCK:, :] = jnp.where(k_mask, k_rep, 0.0).astype(BF16)
            vz_scr[kv, slot, ATTN_BLOCK:, :] = jnp.where(v_mask, v_rep, 1.0).astype(BF16)

    def split(u):
        if isinstance(u, int):
            return u // n_blk, u % n_blk
        return lax.div(u, n_blk), lax.rem(u, n_blk)

    def scores(u):
        kv, i = split(u)
        row0 = i * ATTN_BLOCK if isinstance(i, int) else pl.multiple_of(i * ATTN_BLOCK, ATTN_BLOCK)
        first = jnp.logical_and(ti == 0, i == 0).astype(jnp.int32)
        q_stack = jnp.concatenate(
            [q_scr[kv * PAIRS_PER_KV + n, pl.ds(row0, ATTN_BLOCK), :] for n in range(PAIRS_PER_KV)], axis=0)
        for slot in range(2):
            sink = jnp.concatenate(
                [jnp.full((ATTN_BLOCK, LANES), sinks_ref[2 * (kv * PAIRS_PER_KV + n) + slot], F32)
                 for n in range(PAIRS_PER_KV)], axis=0)
            s = lax.dot_general(q_stack, kz_scr[kv, slot, pl.ds(row0, window), :], (((1,), (1,)), ((), ())),
                                preferred_element_type=F32) + bias_scr[first]
            mx = jnp.maximum(jnp.broadcast_to(jnp.max(s, axis=-1, keepdims=True), (stack, LANES)), sink)
            s_scr[u % 2, slot] = s
            mx_scr[u % 2, slot] = mx
            es_scr[u % 2, slot] = jnp.exp(sink - mx)

    def values(u):
        kv, i = split(u)
        row0 = i * ATTN_BLOCK if isinstance(i, int) else pl.multiple_of(i * ATTN_BLOCK, ATTN_BLOCK)
        outs = []
        for slot in range(2):
            mx = mx_scr[u % 2, slot]
            p = jnp.exp(s_scr[u % 2, slot] - jnp.concatenate([mx, mx], axis=1)).astype(BF16)
            outs.append(jnp.dot(p, vz_scr[kv, slot, pl.ds(row0, window), :], preferred_element_type=F32))
        low = lax.broadcasted_iota(jnp.int32, (1, LANES), 1) < HEAD_DIM
        num = jnp.where(low, outs[0], outs[1])
        sums = pltpu.roll(jnp.where(low, outs[1], outs[0]), HEAD_DIM, axis=1)
        den = sums + jnp.where(low, es_scr[u % 2, 0], es_scr[u % 2, 1])
        o = num / den
        for n in range(PAIRS_PER_KV):
            o_scr[kv * PAIRS_PER_KV + n, pl.ds(row0, ATTN_BLOCK), :] = o[n * ATTN_BLOCK:(n + 1) * ATTN_BLOCK]

    scores(0)

    def unit_step(u, carry):
        values(u - 1)
        scores(u)
        return carry

    lax.fori_loop(1, n_units, unit_step, 0)
    values(n_units - 1)

    gate = proj_scr[:, _G0:]
    attn = jnp.concatenate([o_scr[r] for r in range(N_Q_HEADS // 2)], axis=1)
    act = (attn * (gate * _sigmoid(gate))).astype(BF16)
    y = x + jnp.dot(act, wout_ref[...], preferred_element_type=F32)
    if final_norm:
        y = _rms_scale(y, fin_ref[...])
    o_ref[...] = y


def _attn_layer(x_bt, norm, w_in, sinks, w_out, cos, sin, fin, *, final_norm):
    bsz, seq, dm = x_bt.shape
    rows = min(ATTN_ROWS, seq)
    stack = PAIRS_PER_KV * ATTN_BLOCK
    win = _attn_weight_layout(w_in.astype(F32))
    row_spec = pl.BlockSpec((None, rows, dm), lambda b, t: (b, t, 0))
    tab_spec = pl.BlockSpec((rows, LANES), lambda b, t: (t, 0))
    return pl.pallas_call(
        functools.partial(_attn_layer_kernel, rows=rows, final_norm=final_norm),
        out_shape=jax.ShapeDtypeStruct((bsz, seq, dm), F32),
        grid=(bsz, seq // rows),
        in_specs=[
            pl.BlockSpec(memory_space=pltpu.SMEM),
            row_spec,
            _const_spec((1, dm)),
            _const_spec((dm, _PROJ_COLS)),
            tab_spec,
            tab_spec,
            _const_spec((BRANCH, dm)),
            _const_spec((1, dm)),
        ],
        out_specs=row_spec,
        scratch_shapes=[
            pltpu.VMEM((rows, _PROJ_COLS), F32),
            pltpu.VMEM((N_Q_HEADS // 2, rows, LANES), BF16),
            pltpu.VMEM((N_KV_HEADS, 2, ATTN_BLOCK + rows, LANES), BF16),
            pltpu.VMEM((N_KV_HEADS, 2, ATTN_BLOCK + rows, LANES), BF16),
            pltpu.VMEM((N_Q_HEADS // 2, rows, LANES), F32),
            pltpu.VMEM((2, stack, 2 * ATTN_BLOCK), F32),
            pltpu.VMEM((2, 2, stack, 2 * ATTN_BLOCK), F32),
            pltpu.VMEM((2, 2, stack, LANES), F32),
            pltpu.VMEM((2, 2, stack, LANES), F32),
        ],
        compiler_params=pltpu.CompilerParams(dimension_semantics=("arbitrary", "arbitrary"), vmem_limit_bytes=VMEM_LIMIT),
        name="attn_layer",
    )(sinks.astype(F32), x_bt, norm.astype(F32).reshape(1, dm), win, cos, sin, w_out.astype(BF16),
      fin.astype(F32).reshape(1, dm))


def _to_time_major(x_bt):
    bsz, seq, dm = x_bt.shape
    return jnp.transpose(x_bt, (1, 0, 2)).reshape(seq * bsz, dm)


def _to_batch_major(x_tb, bsz):
    rows, dm = x_tb.shape
    return jnp.transpose(x_tb.reshape(rows // bsz, bsz, dm), (1, 0, 2))


def kernel(x, l0_norm, l0_w_in, l0_a_re, l0_a_im, l0_log_step, l0_b_re, l0_b_im, l0_c_re, l0_c_im, l0_d, l0_w_glu, l0_b_glu, l0_w_out, l1_norm, l1_w_in, l1_sinks, l1_w_out, l2_norm, l2_w_in, l2_a_re, l2_a_im, l2_log_step, l2_b_re, l2_b_im, l2_c_re, l2_c_im, l2_d, l2_w_glu, l2_b_glu, l2_w_out, l3_norm, l3_w_in, l3_sinks, l3_w_out, final_norm):
    bsz, seq, _ = x.shape
    assert bsz == BATCH
    cos, sin = _rope_tables(seq)
    h = x.astype(F32)
    h = _ssm_layer(_to_time_major(h), l0_norm, l0_w_in, l0_a_re, l0_a_im, l0_log_step, l0_b_re, l0_b_im,
                   l0_c_re, l0_c_im, l0_d, l0_w_glu, l0_b_glu, l0_w_out)
    h = _attn_layer(_to_batch_major(h, bsz), l1_norm, l1_w_in, l1_sinks, l1_w_out, cos, sin, final_norm,
                    final_norm=False)
    h = _ssm_layer(_to_time_major(h), l2_norm, l2_w_in, l2_a_re, l2_a_im, l2_log_step, l2_b_re, l2_b_im,
                   l2_c_re, l2_c_im, l2_d, l2_w_glu, l2_b_glu, l2_w_out)
    h = _attn_layer(_to_batch_major(h, bsz), l3_norm, l3_w_in, l3_sinks, l3_w_out, cos, sin, final_norm,
                    final_norm=True)
    return h.astype(x.dtype)
```

```python
import functools
import math

import numpy as np
import jax
import jax.numpy as jnp
from jax import lax
from jax.experimental import pallas as pl
from jax.experimental.pallas import tpu as pltpu

D_MODEL = 1024
BATCH = 8
BRANCH = D_MODEL
SSM_GROUP = 16
SSM_GROUPS = BRANCH // SSM_GROUP
SSM_STATE = 64
HEAD_DIM = 64
N_Q_HEADS = BRANCH // HEAD_DIM
N_KV_HEADS = 2
ATTN_BLOCK = 128
ROPE_THETA = 10000.0
NORM_EPS = 1e-5
NEG_INF = -1e30

LANES = 128
MXU_TILE = 256
GROUPS_PER_TILE = MXU_TILE // SSM_GROUP
STATE_COLS = GROUPS_PER_TILE * SSM_STATE
N_CH_TILES = BRANCH // MXU_TILE
HEADS_PER_TILE = MXU_TILE // HEAD_DIM
HALF = HEAD_DIM // 2

SSM_T_BLOCK = 64
ATTN_ROWS = 512
VMEM_LIMIT = 56 * 1024 * 1024

F32 = jnp.float32
BF16 = jnp.bfloat16


def _sigmoid(v):
    return 1.0 / (1.0 + jnp.exp(-v))


def _gelu_tanh(v):
    c = math.sqrt(2.0 / math.pi)
    return 0.5 * v * (1.0 + jnp.tanh(c * (v + 0.044715 * (v * v * v))))


def _rms_scale(x, g):
    ms = jnp.mean(x * x, axis=-1, keepdims=True)
    return x * lax.rsqrt(ms + NORM_EPS) * g


def _const_spec(shape):
    nd = len(shape)
    return pl.BlockSpec(shape, lambda *_: (0,) * nd, pipeline_mode=pl.Buffered(1))


def _ssm_prep_kernel(are_ref, aim_ref, ls_ref, bre_ref, bim_ref, abr_ref, abi_ref, bbr_ref, bbi_ref):
    lr = are_ref[...]
    li = aim_ref[...]
    step = jnp.exp(ls_ref[...])
    mag = jnp.exp(lr * step)
    ar = mag * jnp.cos(li * step)
    ai = mag * jnp.sin(li * step)
    xr = ar - 1.0
    den = lr * lr + li * li
    cr = (xr * lr + ai * li) / den
    ci = (ai * lr - xr * li) / den
    br = bre_ref[...]
    bi = bim_ref[...]
    abr_ref[...] = ar
    abi_ref[...] = ai
    bbr_ref[...] = cr * br - ci * bi
    bbi_ref[...] = cr * bi + ci * br


def _ssm_prep(a_re, a_im, log_step, b_re, b_im):
    g, p, c = b_re.shape
    rep = lambda a: jnp.repeat(a.astype(F32), c, axis=1)
    ls = jnp.broadcast_to(log_step.astype(F32)[:, None], (g, p * c))
    shp = jax.ShapeDtypeStruct((g, p * c), F32)
    abr, abi, bbr, bbi = pl.pallas_call(
        _ssm_prep_kernel, out_shape=(shp, shp, shp, shp), name="ssm_prep",
    )(rep(a_re), rep(a_im), ls, b_re.astype(F32).reshape(g, p * c), b_im.astype(F32).reshape(g, p * c))
    a_bar_re = abr.reshape(g, p, c)[:, :, 0]
    a_bar_im = abi.reshape(g, p, c)[:, :, 0]
    return a_bar_re, a_bar_im, bbr.reshape(g, p, c), bbi.reshape(g, p, c)


def _ssm_matrices(a_bar_re, a_bar_im, bb_re, bb_im, c_re, c_im):
    eye = jnp.eye(GROUPS_PER_TILE, dtype=F32)
    bb = jnp.stack([bb_re, bb_im]).reshape(2, N_CH_TILES, GROUPS_PER_TILE, SSM_STATE, SSM_GROUP)
    bm = jnp.einsum("ab,rjbpc->jacrbp", eye, bb).reshape(N_CH_TILES, MXU_TILE, 2 * STATE_COLS)
    cc = jnp.stack([c_re.astype(F32), -c_im.astype(F32)]).reshape(2, N_CH_TILES, GROUPS_PER_TILE, SSM_GROUP, SSM_STATE)
    cm = jnp.einsum("ab,rjacp->jrapbc", eye, cc).reshape(N_CH_TILES, 2 * STATE_COLS, MXU_TILE)
    ar8 = jnp.broadcast_to(a_bar_re.reshape(1, -1), (BATCH, SSM_GROUPS * SSM_STATE))
    ai8 = jnp.broadcast_to(a_bar_im.reshape(1, -1), (BATCH, SSM_GROUPS * SSM_STATE))
    return bm.astype(BF16), cm.astype(BF16), ar8, ai8


def _ssm_layer_kernel(x_hbm, nrm_ref, win_ref, bm_ref, cm_ref, ar_ref, ai_ref, d_ref, wglu_ref, bglu_ref,
                      wout_ref, o_hbm, xin_scr, xout_scr, in_sem, out_sem, h_scr, proj_scr, bu_scr, y_scr,
                      *, t_block, n_steps):
    i = pl.program_id(0)
    slot = i % 2
    tm = t_block * BATCH

    def in_copy(step, buf, b):
        return pltpu.make_async_copy(x_hbm.at[b, pl.ds(step * t_block, t_block), :], xin_scr.at[buf, :, b, :],
                                     in_sem.at[buf, b])

    def out_copy(step, buf, b):
        return pltpu.make_async_copy(xout_scr.at[buf, :, b, :], o_hbm.at[b, pl.ds(step * t_block, t_block), :],
                                     out_sem.at[buf, b])

    @pl.when(i == 0)
    def _():
        h_scr[...] = jnp.zeros_like(h_scr)
        for b in range(BATCH):
            in_copy(0, 0, b).start()

    @pl.when(i + 1 < n_steps)
    def _():
        for b in range(BATCH):
            in_copy(i + 1, 1 - slot, b).start()

    for b in range(BATCH):
        in_copy(i, slot, b).wait()

    x = xin_scr[slot].reshape(tm, D_MODEL)
    hn = _rms_scale(x, nrm_ref[...]).astype(BF16)
    proj_scr[...] = jnp.dot(hn, win_ref[...], preferred_element_type=F32)

    scan_w = STATE_COLS // 2
    for j in range(N_CH_TILES):
        ch = slice(j * MXU_TILE, (j + 1) * MXU_TILE)
        bu = bu_scr.at[j % 2]
        u_j = proj_scr[:, ch]
        bu[...] = jnp.dot(u_j.astype(BF16), bm_ref[j], preferred_element_type=F32)
        for k in range(STATE_COLS // scan_w):
            re_cols = slice(k * scan_w, (k + 1) * scan_w)
            im_cols = slice(STATE_COLS + k * scan_w, STATE_COLS + (k + 1) * scan_w)
            a_cols = slice(j * STATE_COLS + k * scan_w, j * STATE_COLS + (k + 1) * scan_w)
            h_re = slice(j * 2 * STATE_COLS + k * scan_w, j * 2 * STATE_COLS + (k + 1) * scan_w)
            h_im = slice(j * 2 * STATE_COLS + STATE_COLS + k * scan_w, j * 2 * STATE_COLS + STATE_COLS + (k + 1) * scan_w)
            ar = ar_ref[:, a_cols]
            ai = ai_ref[:, a_cols]
            hr = h_scr[:, h_re]
            hi = h_scr[:, h_im]
            for t in range(t_block):
                rows = slice(t * BATCH, (t + 1) * BATCH)
                hr, hi = (ar * hr - ai * hi + bu[rows, re_cols], ai * hr + ar * hi + bu[rows, im_cols])
                bu[rows, re_cols] = hr
                bu[rows, im_cols] = hi
            h_scr[:, h_re] = hr
            h_scr[:, h_im] = hi
        y_j = jnp.dot(bu[...].astype(BF16), cm_ref[j], preferred_element_type=F32)
        y_scr[:, ch] = y_j + d_ref[:, ch] * u_j

    z = _gelu_tanh(y_scr[...])
    glu = jnp.dot(z.astype(BF16), wglu_ref[...], preferred_element_type=F32) + bglu_ref[...]
    z = z * _sigmoid(glu)
    gate = proj_scr[:, BRANCH:]
    act = (z * (gate * _sigmoid(gate))).astype(BF16)
    out = x + jnp.dot(act, wout_ref[...], preferred_element_type=F32)

    @pl.when(i >= 2)
    def _():
        for b in range(BATCH):
            out_copy(i - 2, slot, b).wait()

    xout_scr[slot] = out.reshape(t_block, BATCH, D_MODEL)
    for b in range(BATCH):
        out_copy(i, slot, b).start()

    @pl.when(i == n_steps - 1)
    def _():
        if n_steps >= 2:
            for b in range(BATCH):
                out_copy(i - 1, 1 - slot, b).wait()
        for b in range(BATCH):
            out_copy(i, slot, b).wait()


def _ssm_layer(x_bt, norm, w_in, a_re, a_im, log_step, b_re, b_im, c_re, c_im, d, w_glu, b_glu, w_out):
    bsz, seq, dm = x_bt.shape
    t_block = min(SSM_T_BLOCK, seq)
    n_steps = seq // t_block
    tm = t_block * BATCH
    a_bar_re, a_bar_im, bb_re, bb_im = _ssm_prep(a_re, a_im, log_step, b_re, b_im)
    bm, cm, ar8, ai8 = _ssm_matrices(a_bar_re, a_bar_im, bb_re, bb_im, c_re, c_im)
    n_state = SSM_GROUPS * SSM_STATE
    return pl.pallas_call(
        functools.partial(_ssm_layer_kernel, t_block=t_block, n_steps=n_steps),
        out_shape=jax.ShapeDtypeStruct((bsz, seq, dm), F32),
        grid=(n_steps,),
        in_specs=[
            pl.BlockSpec(memory_space=pl.ANY),
            _const_spec((1, dm)),
            _const_spec((dm, 2 * BRANCH)),
            _const_spec(bm.shape),
            _const_spec(cm.shape),
            _const_spec((BATCH, n_state)),
            _const_spec((BATCH, n_state)),
            _const_spec((1, BRANCH)),
            _const_spec((BRANCH, BRANCH)),
            _const_spec((1, BRANCH)),
            _const_spec((BRANCH, dm)),
        ],
        out_specs=pl.BlockSpec(memory_space=pl.ANY),
        scratch_shapes=[
            pltpu.VMEM((2, t_block, BATCH, dm), F32),
            pltpu.VMEM((2, t_block, BATCH, dm), F32),
            pltpu.SemaphoreType.DMA((2, BATCH)),
            pltpu.SemaphoreType.DMA((2, BATCH)),
            pltpu.VMEM((BATCH, 2 * n_state), F32),
            pltpu.VMEM((tm, 2 * BRANCH), F32),
            pltpu.VMEM((2, tm, 2 * STATE_COLS), F32),
            pltpu.VMEM((tm, BRANCH), F32),
        ],
        compiler_params=pltpu.CompilerParams(dimension_semantics=("arbitrary",), vmem_limit_bytes=VMEM_LIMIT),
        name="ssm_layer",
    )(x_bt, norm.astype(F32).reshape(1, dm), w_in.astype(BF16), bm, cm, ar8, ai8,
      d.astype(F32).reshape(1, BRANCH), w_glu.astype(BF16), b_glu.astype(F32).reshape(1, BRANCH), w_out.astype(BF16))


def _rope_kernel(freq_ref, cos_ref, sin_ref):
    rows = cos_ref.shape[0]
    pos = (lax.broadcasted_iota(jnp.int32, cos_ref.shape, 0) + pl.program_id(0) * rows).astype(F32)
    ang = pos * freq_ref[...]
    lane = lax.broadcasted_iota(jnp.int32, cos_ref.shape, 1)
    sin = jnp.sin(ang)
    cos_ref[...] = jnp.cos(ang)
    sin_ref[...] = jnp.where(lane < LANES // 2, -sin, sin)


def _rope_tables(seq):
    inv_freq = (np.float32(ROPE_THETA) ** (-np.arange(0, HEAD_DIM, 2, dtype=np.float32) / np.float32(HEAD_DIM))).astype(np.float32)
    freq = jnp.asarray(np.tile(inv_freq, LANES // HALF).reshape(1, LANES))
    blk = min(1024, seq)
    spec = pl.BlockSpec((blk, LANES), lambda i: (i, 0))
    shp = jax.ShapeDtypeStruct((seq, LANES), F32)
    return pl.pallas_call(
        _rope_kernel, out_shape=(shp, shp), grid=(seq // blk,),
        in_specs=[pl.BlockSpec((1, LANES), lambda i: (0, 0))], out_specs=(spec, spec), name="rope_tables",
    )(freq)


def _attn_weight_layout(w_in):
    wq = w_in[:, :BRANCH] * (HEAD_DIM ** -0.5)
    wk = w_in[:, BRANCH:BRANCH + N_KV_HEADS * HEAD_DIM]
    wv = w_in[:, BRANCH + N_KV_HEADS * HEAD_DIM:BRANCH + 2 * N_KV_HEADS * HEAD_DIM]
    wg = w_in[:, BRANCH + 2 * N_KV_HEADS * HEAD_DIM:]
    dm = w_in.shape[0]
    wq = wq.reshape(dm, N_Q_HEADS // 2, 2, 2, HALF).transpose(0, 1, 3, 2, 4).reshape(dm, BRANCH)
    wk = jnp.broadcast_to(wk.reshape(dm, N_KV_HEADS, 2, 1, HALF), (dm, N_KV_HEADS, 2, 2, HALF))
    wk = wk.reshape(dm, N_KV_HEADS * LANES)
    wv = jnp.broadcast_to(wv.reshape(dm, N_KV_HEADS, 1, HEAD_DIM), (dm, N_KV_HEADS, 2, HEAD_DIM))
    wv = wv.reshape(dm, N_KV_HEADS * LANES)
    return jnp.concatenate([wq, wk, wv, wg], axis=1).astype(BF16)


_Q0 = 0
_K0 = BRANCH
_V0 = BRANCH + N_KV_HEADS * LANES
_G0 = BRANCH + 2 * N_KV_HEADS * LANES
_PROJ_COLS = _G0 + BRANCH
PAIRS_PER_KV = N_Q_HEADS // 2 // N_KV_HEADS


def _attn_layer_kernel(sinks_ref, x_ref, nrm_ref, win_ref, cos_ref, sin_ref, wout_ref, fin_ref, o_ref,
                       proj_scr, q_scr, kz_scr, vz_scr, o_scr, bias_scr, s_scr, mx_scr, es_scr, *, rows, final_norm):
    ti = pl.program_id(1)
    n_blk = rows // ATTN_BLOCK
    n_units = N_KV_HEADS * n_blk
    stack = PAIRS_PER_KV * ATTN_BLOCK
    window = 2 * ATTN_BLOCK

    @pl.when((pl.program_id(0) == 0) & (ti == 0))
    def _():
        qi = lax.broadcasted_iota(jnp.int32, (stack, window), 0) % ATTN_BLOCK
        kj = lax.broadcasted_iota(jnp.int32, (stack, window), 1)
        dist = qi + ATTN_BLOCK - kj
        band = (dist >= 0) & (dist < ATTN_BLOCK)
        bias_scr[0] = jnp.where(band, 0.0, NEG_INF)
        bias_scr[1] = jnp.where(band & (kj >= ATTN_BLOCK), 0.0, NEG_INF)

    @pl.when(ti == 0)
    def _():
        kz_scr[:, :, :ATTN_BLOCK, :] = jnp.zeros((N_KV_HEADS, 2, ATTN_BLOCK, LANES), BF16)
        vz_scr[:, :, :ATTN_BLOCK, :] = jnp.zeros((N_KV_HEADS, 2, ATTN_BLOCK, LANES), BF16)

    @pl.when(ti > 0)
    def _():
        kz_scr[:, :, :ATTN_BLOCK, :] = kz_scr[:, :, rows:, :]
        vz_scr[:, :, :ATTN_BLOCK, :] = vz_scr[:, :, rows:, :]

    x = x_ref[...]
    hn = _rms_scale(x, nrm_ref[...]).astype(BF16)
    proj_scr[...] = jnp.dot(hn, win_ref[...], preferred_element_type=F32)

    cos = cos_ref[...]
    sin = sin_ref[...]

    def rope(t):
        return t * cos + pltpu.roll(t, LANES // 2, axis=1) * sin

    for r in range(N_Q_HEADS // 2):
        q_scr[r] = rope(proj_scr[:, _Q0 + r * LANES:_Q0 + (r + 1) * LANES]).astype(BF16)

    lane = lax.broadcasted_iota(jnp.int32, (1, LANES), 1)
    for kv in range(N_KV_HEADS):
        k_rep = rope(proj_scr[:, _K0 + kv * LANES:_K0 + (kv + 1) * LANES])
        v_rep = proj_scr[:, _V0 + kv * LANES:_V0 + (kv + 1) * LANES]
        for slot in range(2):
            k_mask = (lane // HALF) % 2 == slot
            v_mask = lane // HEAD_DIM == slot
            kz_scr[kv, slot, ATTN_BLOCK:, :] = jnp.where(k_mask, k_rep, 0.0).astype(BF16)
            vz_scr[kv, slot, ATTN_BLOCK:, :] = jnp.where(v_mask, v_rep, 1.0).astype(BF16)

    def split(u):
        if isinstance(u, int):
            return u // n_blk, u % n_blk
        return lax.div(u, n_blk), lax.rem(u, n_blk)

    def scores(u):
        kv, i = split(u)
        row0 = i * ATTN_BLOCK if isinstance(i, int) else pl.multiple_of(i * ATTN_BLOCK, ATTN_BLOCK)
        first = jnp.logical_and(ti == 0, i == 0).astype(jnp.int32)
        q_stack = jnp.concatenate(
            [q_scr[kv * PAIRS_PER_KV + n, pl.ds(row0, ATTN_BLOCK), :] for n in range(PAIRS_PER_KV)], axis=0)
        for slot in range(2):
            sink = jnp.concatenate(
                [jnp.full((ATTN_BLOCK, LANES), sinks_ref[2 * (kv * PAIRS_PER_KV + n) + slot], F32)
                 for n in range(PAIRS_PER_KV)], axis=0)
            s = lax.dot_general(q_stack, kz_scr[kv, slot, pl.ds(row0, window), :], (((1,), (1,)), ((), ())),
                                preferred_element_type=F32) + bias_scr[first]
            mx = jnp.maximum(jnp.broadcast_to(jnp.max(s, axis=-1, keepdims=True), (stack, LANES)), sink)
            s_scr[u % 2, slot] = s
            mx_scr[u % 2, slot] = mx
            es_scr[u % 2, slot] = sink - mx

    def values(u):
        kv, i = split(u)
        row0 = i * ATTN_BLOCK if isinstance(i, int) else pl.multiple_of(i * ATTN_BLOCK, ATTN_BLOCK)
        outs = []
        for slot in range(2):
            mx = mx_scr[u % 2, slot]
            p = jnp.exp(s_scr[u % 2, slot] - jnp.concatenate([mx, mx], axis=1)).astype(BF16)
            outs.append(jnp.dot(p, vz_scr[kv, slot, pl.ds(row0, window), :], preferred_element_type=F32))
        low = lax.broadcasted_iota(jnp.int32, (1, LANES), 1) < HEAD_DIM
        num = jnp.where(low, outs[0], outs[1])
        sums = pltpu.roll(jnp.where(low, outs[1], outs[0]), HEAD_DIM, axis=1)
        den = sums + jnp.exp(jnp.where(low, es_scr[u % 2, 0], es_scr[u % 2, 1]))
        o = num / den
        for n in range(PAIRS_PER_KV):
            o_scr[kv * PAIRS_PER_KV + n, pl.ds(row0, ATTN_BLOCK), :] = o[n * ATTN_BLOCK:(n + 1) * ATTN_BLOCK]

    scores(0)

    def unit_step(u, carry):
        values(u - 1)
        scores(u)
        return carry

    lax.fori_loop(1, n_units, unit_step, 0)
    values(n_units - 1)

    gate = proj_scr[:, _G0:]
    attn = jnp.concatenate([o_scr[r] for r in range(N_Q_HEADS // 2)], axis=1)
    act = (attn * (gate * _sigmoid(gate))).astype(BF16)
    y = x + jnp.dot(act, wout_ref[...], preferred_element_type=F32)
    if final_norm:
        y = _rms_scale(y, fin_ref[...])
    o_ref[...] = y


def _attn_layer(x_bt, norm, w_in, sinks, w_out, cos, sin, fin, *, final_norm):
    bsz, seq, dm = x_bt.shape
    rows = min(ATTN_ROWS, seq)
    stack = PAIRS_PER_KV * ATTN_BLOCK
    win = _attn_weight_layout(w_in.astype(F32))
    row_spec = pl.BlockSpec((None, rows, dm), lambda b, t: (b, t, 0))
    tab_spec = pl.BlockSpec((rows, LANES), lambda b, t: (t, 0))
    return pl.pallas_call(
        functools.partial(_attn_layer_kernel, rows=rows, final_norm=final_norm),
        out_shape=jax.ShapeDtypeStruct((bsz, seq, dm), F32),
        grid=(bsz, seq // rows),
        in_specs=[
            pl.BlockSpec(memory_space=pltpu.SMEM),
            row_spec,
            _const_spec((1, dm)),
            _const_spec((dm, _PROJ_COLS)),
            tab_spec,
            tab_spec,
            _const_spec((BRANCH, dm)),
            _const_spec((1, dm)),
        ],
        out_specs=row_spec,
        scratch_shapes=[
            pltpu.VMEM((rows, _PROJ_COLS), F32),
            pltpu.VMEM((N_Q_HEADS // 2, rows, LANES), BF16),
            pltpu.VMEM((N_KV_HEADS, 2, ATTN_BLOCK + rows, LANES), BF16),
            pltpu.VMEM((N_KV_HEADS, 2, ATTN_BLOCK + rows, LANES), BF16),
            pltpu.VMEM((N_Q_HEADS // 2, rows, LANES), F32),
            pltpu.VMEM((2, stack, 2 * ATTN_BLOCK), F32),
            pltpu.VMEM((2, 2, stack, 2 * ATTN_BLOCK), F32),
            pltpu.VMEM((2, 2, stack, LANES), F32),
            pltpu.VMEM((2, 2, stack, LANES), F32),
        ],
        compiler_params=pltpu.CompilerParams(dimension_semantics=("arbitrary", "arbitrary"), vmem_limit_bytes=VMEM_LIMIT),
        name="attn_layer",
    )(sinks.astype(F32), x_bt, norm.astype(F32).reshape(1, dm), win, cos, sin, w_out.astype(BF16),
      fin.astype(F32).reshape(1, dm))


def kernel(x, l0_norm, l0_w_in, l0_a_re, l0_a_im, l0_log_step, l0_b_re, l0_b_im, l0_c_re, l0_c_im, l0_d, l0_w_glu, l0_b_glu, l0_w_out, l1_norm, l1_w_in, l1_sinks, l1_w_out, l2_norm, l2_w_in, l2_a_re, l2_a_im, l2_log_step, l2_b_re, l2_b_im, l2_c_re, l2_c_im, l2_d, l2_w_glu, l2_b_glu, l2_w_out, l3_norm, l3_w_in, l3_sinks, l3_w_out, final_norm):
    bsz, seq, _ = x.shape
    assert bsz == BATCH
    cos, sin = _rope_tables(seq)
    h = x.astype(F32)
    h = _ssm_layer(h, l0_norm, l0_w_in, l0_a_re, l0_a_im, l0_log_step, l0_b_re, l0_b_im,
                   l0_c_re, l0_c_im, l0_d, l0_w_glu, l0_b_glu, l0_w_out)
    h = _attn_layer(h, l1_norm, l1_w_in, l1_sinks, l1_w_out, cos, sin, final_norm, final_norm=False)
    h = _ssm_layer(h, l2_norm, l2_w_in, l2_a_re, l2_a_im, l2_log_step, l2_b_re, l2_b_im,
                   l2_c_re, l2_c_im, l2_d, l2_w_glu, l2_b_glu, l2_w_out)
    h = _attn_layer(h, l3_norm, l3_w_in, l3_sinks, l3_w_out, cos, sin, final_norm, final_norm=True)
    return h.astype(x.dtype)
```

```python
import functools
import math

import numpy as np
import jax
import jax.numpy as jnp
from jax import lax
from jax.experimental import pallas as pl
from jax.experimental.pallas import tpu as pltpu

D_MODEL = 1024
BATCH = 8
BRANCH = D_MODEL
SSM_GROUP = 16
SSM_GROUPS = BRANCH // SSM_GROUP
SSM_STATE = 64
HEAD_DIM = 64
N_Q_HEADS = BRANCH // HEAD_DIM
N_KV_HEADS = 2
ATTN_BLOCK = 128
ROPE_THETA = 10000.0
NORM_EPS = 1e-5
NEG_INF = -1e30

LANES = 128
GROUPS_PER_OCTET = LANES // SSM_GROUP
N_OCTETS = BRANCH // LANES
OCTET_STATE = GROUPS_PER_OCTET * SSM_STATE
HALF = HEAD_DIM // 2

SSM_T_BLOCK = 64
ATTN_ROWS = 512
VMEM_LIMIT = 56 * 1024 * 1024

F32 = jnp.float32
BF16 = jnp.bfloat16


def _sigmoid(v):
    return 1.0 / (1.0 + jnp.exp(-v))


def _gelu_tanh(v):
    c = math.sqrt(2.0 / math.pi)
    return 0.5 * v * (1.0 + jnp.tanh(c * (v + 0.044715 * (v * v * v))))


def _rms_scale(x, g):
    ms = jnp.mean(x * x, axis=-1, keepdims=True)
    return x * lax.rsqrt(ms + NORM_EPS) * g


def _const_spec(shape):
    nd = len(shape)
    return pl.BlockSpec(shape, lambda *_: (0,) * nd, pipeline_mode=pl.Buffered(1))


def _ssm_prep_kernel(are_ref, aim_ref, ls_ref, bre_ref, bim_ref, abr_ref, abi_ref, bbr_ref, bbi_ref):
    lr = are_ref[...]
    li = aim_ref[...]
    step = jnp.exp(ls_ref[...])
    mag = jnp.exp(lr * step)
    ar = mag * jnp.cos(li * step)
    ai = mag * jnp.sin(li * step)
    xr = ar - 1.0
    den = lr * lr + li * li
    cr = (xr * lr + ai * li) / den
    ci = (ai * lr - xr * li) / den
    br = bre_ref[...]
    bi = bim_ref[...]
    abr_ref[...] = ar
    abi_ref[...] = ai
    bbr_ref[...] = cr * br - ci * bi
    bbi_ref[...] = cr * bi + ci * br


def _ssm_prep(a_re, a_im, log_step, b_re, b_im):
    g, p, c = b_re.shape
    rep = lambda a: jnp.repeat(a.astype(F32), c, axis=1)
    ls = jnp.broadcast_to(log_step.astype(F32)[:, None], (g, p * c))
    shp = jax.ShapeDtypeStruct((g, p * c), F32)
    abr, abi, bbr, bbi = pl.pallas_call(
        _ssm_prep_kernel, out_shape=(shp, shp, shp, shp), name="ssm_prep",
    )(rep(a_re), rep(a_im), ls, b_re.astype(F32).reshape(g, p * c), b_im.astype(F32).reshape(g, p * c))
    a_bar_re = abr.reshape(g, p, c)[:, :, 0]
    a_bar_im = abi.reshape(g, p, c)[:, :, 0]
    return a_bar_re, a_bar_im, bbr.reshape(g, p, c), bbi.reshape(g, p, c)


def _ssm_matrices(a_re, a_im, bb_re, bb_im, c_re, c_im):
    hi_p = lax.Precision.HIGHEST
    cr = c_re.astype(F32)
    ci = c_im.astype(F32)
    ar = a_re[:, :, None]
    ai = a_im[:, :, None]
    abr = ar * bb_re - ai * bb_im
    abi = ar * bb_im + ai * bb_re
    a2r = a_re * a_re - a_im * a_im
    a2i = 2.0 * a_re * a_im
    car = cr * a_re[:, None, :] - ci * a_im[:, None, :]
    cai = cr * a_im[:, None, :] + ci * a_re[:, None, :]
    ca2r = cr * a2r[:, None, :] - ci * a2i[:, None, :]
    ca2i = cr * a2i[:, None, :] + ci * a2r[:, None, :]
    k0 = jnp.einsum("gcp,gpd->gcd", cr, bb_re, precision=hi_p) - jnp.einsum("gcp,gpd->gcd", ci, bb_im, precision=hi_p)
    k1 = jnp.einsum("gcp,gpd->gcd", car, bb_re, precision=hi_p) - jnp.einsum("gcp,gpd->gcd", cai, bb_im, precision=hi_p)

    eye = jnp.eye(GROUPS_PER_OCTET, dtype=F32)
    oct_shape = (N_OCTETS, GROUPS_PER_OCTET)
    zero = jnp.zeros_like(k0)
    t = jnp.stack([jnp.stack([k0, k1]), jnp.stack([zero, k0])]).reshape(2, 2, *oct_shape, SSM_GROUP, SSM_GROUP)
    w_direct = jnp.einsum("ab,pdobxy->opaydbx", eye, t).reshape(N_OCTETS, 2 * LANES, 2 * LANES)
    s = jnp.stack([jnp.stack([abr, abi]), jnp.stack([bb_re, bb_im])]).reshape(2, 2, *oct_shape, SSM_STATE, SSM_GROUP)
    w_state = jnp.einsum("ab,drobpy->odayrbp", eye, s).reshape(N_OCTETS, 2 * LANES, 2 * OCTET_STATE)
    m = jnp.stack([jnp.stack([car, -cai]), jnp.stack([ca2r, -ca2i])]).reshape(2, 2, *oct_shape, SSM_GROUP, SSM_STATE)
    w_carry = jnp.einsum("ab,drobxp->orbpdax", eye, m).reshape(N_OCTETS, 2 * OCTET_STATE, 2 * LANES)
    a2r8 = jnp.broadcast_to(a2r.reshape(1, -1), (BATCH, SSM_GROUPS * SSM_STATE))
    a2i8 = jnp.broadcast_to(a2i.reshape(1, -1), (BATCH, SSM_GROUPS * SSM_STATE))
    return w_direct.astype(BF16), w_state.astype(BF16), w_carry.astype(BF16), a2r8, a2i8


def _ssm_layer_kernel(x_hbm, nrm_ref, win_ref, wd_ref, ws_ref, wc_ref, a2r_ref, a2i_ref, d_ref, wglu_ref, bglu_ref,
                      wout_ref, o_hbm, xin_scr, xout_scr, in_sem, out_sem, h_scr, proj_scr, hb_scr, y_scr,
                      *, t_block, n_steps):
    i = pl.program_id(0)
    slot = i % 2
    tm = t_block * BATCH
    n_chunks = t_block // 2
    cr = n_chunks * BATCH

    def in_copy(step, buf, b):
        return pltpu.make_async_copy(x_hbm.at[b, pl.ds(step * t_block, t_block), :], xin_scr.at[buf, :, b, :],
                                     in_sem.at[buf, b])

    def out_copy(step, buf, b):
        return pltpu.make_async_copy(xout_scr.at[buf, :, b, :], o_hbm.at[b, pl.ds(step * t_block, t_block), :],
                                     out_sem.at[buf, b])

    @pl.when(i == 0)
    def _():
        h_scr[...] = jnp.zeros_like(h_scr)
        for b in range(BATCH):
            in_copy(0, 0, b).start()

    @pl.when(i + 1 < n_steps)
    def _():
        for b in range(BATCH):
            in_copy(i + 1, 1 - slot, b).start()

    for b in range(BATCH):
        in_copy(i, slot, b).wait()

    x = xin_scr[slot].reshape(tm, D_MODEL)
    hn = _rms_scale(x, nrm_ref[...]).astype(BF16)
    proj_scr[...] = jnp.dot(hn, win_ref[...], preferred_element_type=F32).reshape(n_chunks, 2, BATCH, 2 * BRANCH)

    for o in range(N_OCTETS):
        ch = slice(o * LANES, (o + 1) * LANES)
        st = slice(o * 2 * OCTET_STATE, (o + 1) * 2 * OCTET_STATE)
        hb = hb_scr.at[o % 2]
        up = jnp.concatenate([proj_scr[:, 0, :, ch].reshape(cr, LANES), proj_scr[:, 1, :, ch].reshape(cr, LANES)],
                             axis=1).astype(BF16)
        yp = jnp.dot(up, wd_ref[o], preferred_element_type=F32)
        hb[BATCH:, :] = jnp.dot(up, ws_ref[o], preferred_element_type=F32)
        hb[:BATCH, :] = h_scr[:, st]
        a2r = a2r_ref[:, o * OCTET_STATE:(o + 1) * OCTET_STATE]
        a2i = a2i_ref[:, o * OCTET_STATE:(o + 1) * OCTET_STATE]
        hr = hb[:BATCH, :OCTET_STATE]
        hi = hb[:BATCH, OCTET_STATE:]
        for k in range(n_chunks):
            rows = slice((k + 1) * BATCH, (k + 2) * BATCH)
            hr, hi = (a2r * hr - a2i * hi + hb[rows, :OCTET_STATE], a2i * hr + a2r * hi + hb[rows, OCTET_STATE:])
            hb[rows, :OCTET_STATE] = hr
            hb[rows, OCTET_STATE:] = hi
        h_scr[:, st] = hb[cr:, :]
        yp = yp + jnp.dot(hb[:cr, :].astype(BF16), wc_ref[o], preferred_element_type=F32)
        y_scr[:, 0, :, ch] = yp[:, :LANES].reshape(n_chunks, BATCH, LANES)
        y_scr[:, 1, :, ch] = yp[:, LANES:].reshape(n_chunks, BATCH, LANES)

    u = proj_scr[:, :, :, :BRANCH].reshape(tm, BRANCH)
    z = _gelu_tanh(y_scr[...].reshape(tm, BRANCH) + d_ref[...] * u)
    glu = jnp.dot(z.astype(BF16), wglu_ref[...], preferred_element_type=F32) + bglu_ref[...]
    z = z * _sigmoid(glu)
    gate = proj_scr[:, :, :, BRANCH:].reshape(tm, BRANCH)
    act = (z * (gate * _sigmoid(gate))).astype(BF16)
    out = x + jnp.dot(act, wout_ref[...], preferred_element_type=F32)

    @pl.when(i >= 2)
    def _():
        for b in range(BATCH):
            out_copy(i - 2, slot, b).wait()

    xout_scr[slot] = out.reshape(t_block, BATCH, D_MODEL)
    for b in range(BATCH):
        out_copy(i, slot, b).start()

    @pl.when(i == n_steps - 1)
    def _():
        if n_steps >= 2:
            for b in range(BATCH):
                out_copy(i - 1, 1 - slot, b).wait()
        for b in range(BATCH):
            out_copy(i, slot, b).wait()


def _ssm_layer(x_bt, norm, w_in, a_re, a_im, log_step, b_re, b_im, c_re, c_im, d, w_glu, b_glu, w_out):
    bsz, seq, dm = x_bt.shape
    t_block = min(SSM_T_BLOCK, seq)
    n_steps = seq // t_block
    n_chunks = t_block // 2
    a_bar_re, a_bar_im, bb_re, bb_im = _ssm_prep(a_re, a_im, log_step, b_re, b_im)
    wd, ws, wc, a2r8, a2i8 = _ssm_matrices(a_bar_re, a_bar_im, bb_re, bb_im, c_re, c_im)
    n_state = SSM_GROUPS * SSM_STATE
    return pl.pallas_call(
        functools.partial(_ssm_layer_kernel, t_block=t_block, n_steps=n_steps),
        out_shape=jax.ShapeDtypeStruct((bsz, seq, dm), F32),
        grid=(n_steps,),
        in_specs=[
            pl.BlockSpec(memory_space=pl.ANY),
            _const_spec((1, dm)),
            _const_spec((dm, 2 * BRANCH)),
            _const_spec(wd.shape),
            _const_spec(ws.shape),
            _const_spec(wc.shape),
            _const_spec((BATCH, n_state)),
            _const_spec((BATCH, n_state)),
            _const_spec((1, BRANCH)),
            _const_spec((BRANCH, BRANCH)),
            _const_spec((1, BRANCH)),
            _const_spec((BRANCH, dm)),
        ],
        out_specs=pl.BlockSpec(memory_space=pl.ANY),
        scratch_shapes=[
            pltpu.VMEM((2, t_block, BATCH, dm), F32),
            pltpu.VMEM((2, t_block, BATCH, dm), F32),
            pltpu.SemaphoreType.DMA((2, BATCH)),
            pltpu.SemaphoreType.DMA((2, BATCH)),
            pltpu.VMEM((BATCH, 2 * n_state), F32),
            pltpu.VMEM((n_chunks, 2, BATCH, 2 * BRANCH), F32),
            pltpu.VMEM((2, (n_chunks + 1) * BATCH, 2 * OCTET_STATE), F32),
            pltpu.VMEM((n_chunks, 2, BATCH, BRANCH), F32),
        ],
        compiler_params=pltpu.CompilerParams(dimension_semantics=("arbitrary",), vmem_limit_bytes=VMEM_LIMIT),
        name="ssm_layer",
    )(x_bt, norm.astype(F32).reshape(1, dm), w_in.astype(BF16), wd, ws, wc, a2r8, a2i8,
      d.astype(F32).reshape(1, BRANCH), w_glu.astype(BF16), b_glu.astype(F32).reshape(1, BRANCH), w_out.astype(BF16))


def _rope_kernel(freq_ref, cos_ref, sin_ref):
    rows = cos_ref.shape[0]
    pos = (lax.broadcasted_iota(jnp.int32, cos_ref.shape, 0) + pl.program_id(0) * rows).astype(F32)
    ang = pos * freq_ref[...]
    lane = lax.broadcasted_iota(jnp.int32, cos_ref.shape, 1)
    sin = jnp.sin(ang)
    cos_ref[...] = jnp.cos(ang)
    sin_ref[...] = jnp.where(lane < LANES // 2, -sin, sin)


def _rope_tables(seq):
    inv_freq = (np.float32(ROPE_THETA) ** (-np.arange(0, HEAD_DIM, 2, dtype=np.float32) / np.float32(HEAD_DIM))).astype(np.float32)
    freq = jnp.asarray(np.tile(inv_freq, LANES // HALF).reshape(1, LANES))
    blk = min(1024, seq)
    spec = pl.BlockSpec((blk, LANES), lambda i: (i, 0))
    shp = jax.ShapeDtypeStruct((seq, LANES), F32)
    return pl.pallas_call(
        _rope_kernel, out_shape=(shp, shp), grid=(seq // blk,),
        in_specs=[pl.BlockSpec((1, LANES), lambda i: (0, 0))], out_specs=(spec, spec), name="rope_tables",
    )(freq)


def _attn_weight_layout(w_in):
    wq = w_in[:, :BRANCH] * (HEAD_DIM ** -0.5)
    wk = w_in[:, BRANCH:BRANCH + N_KV_HEADS * HEAD_DIM]
    wv = w_in[:, BRANCH + N_KV_HEADS * HEAD_DIM:BRANCH + 2 * N_KV_HEADS * HEAD_DIM]
    wg = w_in[:, BRANCH + 2 * N_KV_HEADS * HEAD_DIM:]
    dm = w_in.shape[0]
    wq = wq.reshape(dm, N_Q_HEADS // 2, 2, 2, HALF).transpose(0, 1, 3, 2, 4).reshape(dm, BRANCH)
    wk = jnp.broadcast_to(wk.reshape(dm, N_KV_HEADS, 2, 1, HALF), (dm, N_KV_HEADS, 2, 2, HALF))
    wk = wk.reshape(dm, N_KV_HEADS * LANES)
    wv = jnp.broadcast_to(wv.reshape(dm, N_KV_HEADS, 1, HEAD_DIM), (dm, N_KV_HEADS, 2, HEAD_DIM))
    wv = wv.reshape(dm, N_KV_HEADS * LANES)
    return jnp.concatenate([wq, wk, wv, wg], axis=1).astype(BF16)


_Q0 = 0
_K0 = BRANCH
_V0 = BRANCH + N_KV_HEADS * LANES
_G0 = BRANCH + 2 * N_KV_HEADS * LANES
_PROJ_COLS = _G0 + BRANCH
PAIRS_PER_KV = N_Q_HEADS // 2 // N_KV_HEADS


def _attn_layer_kernel(sinks_ref, x_ref, nrm_ref, win_ref, cos_ref, sin_ref, wout_ref, fin_ref, o_ref,
                       proj_scr, q_scr, kz_scr, vz_scr, o_scr, bias_scr, s_scr, mx_scr, es_scr, *, rows, final_norm):
    ti = pl.program_id(1)
    n_blk = rows // ATTN_BLOCK
    n_units = N_KV_HEADS * n_blk
    stack = PAIRS_PER_KV * ATTN_BLOCK
    window = 2 * ATTN_BLOCK

    @pl.when((pl.program_id(0) == 0) & (ti == 0))
    def _():
        qi = lax.broadcasted_iota(jnp.int32, (stack, window), 0) % ATTN_BLOCK
        kj = lax.broadcasted_iota(jnp.int32, (stack, window), 1)
        dist = qi + ATTN_BLOCK - kj
        band = (dist >= 0) & (dist < ATTN_BLOCK)
        bias_scr[0] = jnp.where(band, 0.0, NEG_INF)
        bias_scr[1] = jnp.where(band & (kj >= ATTN_BLOCK), 0.0, NEG_INF)

    @pl.when(ti == 0)
    def _():
        kz_scr[:, :, :ATTN_BLOCK, :] = jnp.zeros((N_KV_HEADS, 2, ATTN_BLOCK, LANES), BF16)
        vz_scr[:, :, :ATTN_BLOCK, :] = jnp.zeros((N_KV_HEADS, 2, ATTN_BLOCK, LANES), BF16)

    @pl.when(ti > 0)
    def _():
        kz_scr[:, :, :ATTN_BLOCK, :] = kz_scr[:, :, rows:, :]
        vz_scr[:, :, :ATTN_BLOCK, :] = vz_scr[:, :, rows:, :]

    x = x_ref[...]
    hn = _rms_scale(x, nrm_ref[...]).astype(BF16)
    proj_scr[...] = jnp.dot(hn, win_ref[...], preferred_element_type=F32)

    cos = cos_ref[...]
    sin = sin_ref[...]

    def rope(t):
        return t * cos + pltpu.roll(t, LANES // 2, axis=1) * sin

    for r in range(N_Q_HEADS // 2):
        q_scr[r] = rope(proj_scr[:, _Q0 + r * LANES:_Q0 + (r + 1) * LANES]).astype(BF16)

    lane = lax.broadcasted_iota(jnp.int32, (1, LANES), 1)
    for kv in range(N_KV_HEADS):
        k_rep = rope(proj_scr[:, _K0 + kv * LANES:_K0 + (kv + 1) * LANES])
        v_rep = proj_scr[:, _V0 + kv * LANES:_V0 + (kv + 1) * LANES]
        for slot in range(2):
            k_mask = (lane // HALF) % 2 == slot
            v_mask = lane // HEAD_DIM == slot
            kz_scr[kv, slot, ATTN_BLOCK:, :] = jnp.where(k_mask, k_rep, 0.0).astype(BF16)
            vz_scr[kv, slot, ATTN_BLOCK:, :] = jnp.where(v_mask, v_rep, 1.0).astype(BF16)

    def split(u):
        if isinstance(u, int):
            return u // n_blk, u % n_blk
        return lax.div(u, n_blk), lax.rem(u, n_blk)

    def scores(u):
        kv, i = split(u)
        row0 = i * ATTN_BLOCK if isinstance(i, int) else pl.multiple_of(i * ATTN_BLOCK, ATTN_BLOCK)
        first = jnp.logical_and(ti == 0, i == 0).astype(jnp.int32)
        q_stack = jnp.concatenate(
            [q_scr[kv * PAIRS_PER_KV + n, pl.ds(row0, ATTN_BLOCK), :] for n in range(PAIRS_PER_KV)], axis=0)
        for slot in range(2):
            sink = jnp.concatenate(
                [jnp.full((ATTN_BLOCK, LANES), sinks_ref[2 * (kv * PAIRS_PER_KV + n) + slot], F32)
                 for n in range(PAIRS_PER_KV)], axis=0)
            s = lax.dot_general(q_stack, kz_scr[kv, slot, pl.ds(row0, window), :], (((1,), (1,)), ((), ())),
                                preferred_element_type=F32) + bias_scr[first]
            mx = jnp.maximum(jnp.broadcast_to(jnp.max(s, axis=-1, keepdims=True), (stack, LANES)), sink)
            s_scr[u % 2, slot] = s
            mx_scr[u % 2, slot] = mx
            es_scr[u % 2, slot] = sink - mx

    def values(u):
        kv, i = split(u)
        row0 = i * ATTN_BLOCK if isinstance(i, int) else pl.multiple_of(i * ATTN_BLOCK, ATTN_BLOCK)
        outs = []
        for slot in range(2):
            mx = mx_scr[u % 2, slot]
            p = jnp.exp(s_scr[u % 2, slot] - jnp.concatenate([mx, mx], axis=1)).astype(BF16)
            outs.append(jnp.dot(p, vz_scr[kv, slot, pl.ds(row0, window), :], preferred_element_type=F32))
        low = lax.broadcasted_iota(jnp.int32, (1, LANES), 1) < HEAD_DIM
        num = jnp.where(low, outs[0], outs[1])
        sums = pltpu.roll(jnp.where(low, outs[1], outs[0]), HEAD_DIM, axis=1)
        den = sums + jnp.exp(jnp.where(low, es_scr[u % 2, 0], es_scr[u % 2, 1]))
        o = num / den
        for n in range(PAIRS_PER_KV):
            o_scr[kv * PAIRS_PER_KV + n, pl.ds(row0, ATTN_BLOCK), :] = o[n * ATTN_BLOCK:(n + 1) * ATTN_BLOCK]

    scores(0)

    def unit_step(u, carry):
        values(u - 1)
        scores(u)
        return carry

    lax.fori_loop(1, n_units, unit_step, 0)
    values(n_units - 1)

    gate = proj_scr[:, _G0:]
    attn = jnp.concatenate([o_scr[r] for r in range(N_Q_HEADS // 2)], axis=1)
    act = (attn * (gate * _sigmoid(gate))).astype(BF16)
    y = x + jnp.dot(act, wout_ref[...], preferred_element_type=F32)
    if final_norm:
        y = _rms_scale(y, fin_ref[...])
    o_ref[...] = y


def _attn_layer(x_bt, norm, w_in, sinks, w_out, cos, sin, fin, *, final_norm):
    bsz, seq, dm = x_bt.shape
    rows = min(ATTN_ROWS, seq)
    stack = PAIRS_PER_KV * ATTN_BLOCK
    win = _attn_weight_layout(w_in.astype(F32))
    row_spec = pl.BlockSpec((None, rows, dm), lambda b, t: (b, t, 0))
    tab_spec = pl.BlockSpec((rows, LANES), lambda b, t: (t, 0))
    return pl.pallas_call(
        functools.partial(_attn_layer_kernel, rows=rows, final_norm=final_norm),
        out_shape=jax.ShapeDtypeStruct((bsz, seq, dm), F32),
        grid=(bsz, seq // rows),
        in_specs=[
            pl.BlockSpec(memory_space=pltpu.SMEM),
            row_spec,
            _const_spec((1, dm)),
            _const_spec((dm, _PROJ_COLS)),
            tab_spec,
            tab_spec,
            _const_spec((BRANCH, dm)),
            _const_spec((1, dm)),
        ],
        out_specs=row_spec,
        scratch_shapes=[
            pltpu.VMEM((rows, _PROJ_COLS), F32),
            pltpu.VMEM((N_Q_HEADS // 2, rows, LANES), BF16),
            pltpu.VMEM((N_KV_HEADS, 2, ATTN_BLOCK + rows, LANES), BF16),
            pltpu.VMEM((N_KV_HEADS, 2, ATTN_BLOCK + rows, LANES), BF16),
            pltpu.VMEM((N_Q_HEADS // 2, rows, LANES), F32),
            pltpu.VMEM((2, stack, 2 * ATTN_BLOCK), F32),
            pltpu.VMEM((2, 2, stack, 2 * ATTN_BLOCK), F32),
            pltpu.VMEM((2, 2, stack, LANES), F32),
            pltpu.VMEM((2, 2, stack, LANES), F32),
        ],
        compiler_params=pltpu.CompilerParams(dimension_semantics=("arbitrary", "arbitrary"), vmem_limit_bytes=VMEM_LIMIT),
        name="attn_layer",
    )(sinks.astype(F32), x_bt, norm.astype(F32).reshape(1, dm), win, cos, sin, w_out.astype(BF16),
      fin.astype(F32).reshape(1, dm))


def kernel(x, l0_norm, l0_w_in, l0_a_re, l0_a_im, l0_log_step, l0_b_re, l0_b_im, l0_c_re, l0_c_im, l0_d, l0_w_glu, l0_b_glu, l0_w_out, l1_norm, l1_w_in, l1_sinks, l1_w_out, l2_norm, l2_w_in, l2_a_re, l2_a_im, l2_log_step, l2_b_re, l2_b_im, l2_c_re, l2_c_im, l2_d, l2_w_glu, l2_b_glu, l2_w_out, l3_norm, l3_w_in, l3_sinks, l3_w_out, final_norm):
    bsz, seq, _ = x.shape
    assert bsz == BATCH
    cos, sin = _rope_tables(seq)
    h = x.astype(F32)
    h = _ssm_layer(h, l0_norm, l0_w_in, l0_a_re, l0_a_im, l0_log_step, l0_b_re, l0_b_im,
                   l0_c_re, l0_c_im, l0_d, l0_w_glu, l0_b_glu, l0_w_out)
    h = _attn_layer(h, l1_norm, l1_w_in, l1_sinks, l1_w_out, cos, sin, final_norm, final_norm=False)
    h = _ssm_layer(h, l2_norm, l2_w_in, l2_a_re, l2_a_im, l2_log_step, l2_b_re, l2_b_im,
                   l2_c_re, l2_c_im, l2_d, l2_w_glu, l2_b_glu, l2_w_out)
    h = _attn_layer(h, l3_norm, l3_w_in, l3_sinks, l3_w_out, cos, sin, final_norm, final_norm=True)
    return h.astype(x.dtype)
```

```python
import functools
import math

import numpy as np
import jax
import jax.numpy as jnp
from jax import lax
from jax.experimental import pallas as pl
from jax.experimental.pallas import tpu as pltpu

D_MODEL = 1024
BATCH = 8
BRANCH = D_MODEL
SSM_GROUP = 16
SSM_GROUPS = BRANCH // SSM_GROUP
SSM_STATE = 64
HEAD_DIM = 64
N_Q_HEADS = BRANCH // HEAD_DIM
N_KV_HEADS = 2
ATTN_BLOCK = 128
ROPE_THETA = 10000.0
NORM_EPS = 1e-5
NEG_INF = -1e30

LANES = 128
GROUPS_PER_OCTET = LANES // SSM_GROUP
N_OCTETS = BRANCH // LANES
OCTET_STATE = GROUPS_PER_OCTET * SSM_STATE
HALF = HEAD_DIM // 2

SSM_T_BLOCK = 64
ATTN_ROWS = 512
VMEM_LIMIT = 56 * 1024 * 1024

F32 = jnp.float32
BF16 = jnp.bfloat16


def _sigmoid(v):
    return 1.0 / (1.0 + jnp.exp(-v))


def _gelu_tanh(v):
    c = math.sqrt(2.0 / math.pi)
    return 0.5 * v * (1.0 + jnp.tanh(c * (v + 0.044715 * (v * v * v))))


def _rms_scale(x, g):
    ms = jnp.mean(x * x, axis=-1, keepdims=True)
    return x * lax.rsqrt(ms + NORM_EPS) * g


def _const_spec(shape):
    nd = len(shape)
    return pl.BlockSpec(shape, lambda *_: (0,) * nd, pipeline_mode=pl.Buffered(1))


def _ssm_prep_kernel(are_ref, aim_ref, ls_ref, bre_ref, bim_ref, abr_ref, abi_ref, bbr_ref, bbi_ref):
    lr = are_ref[...]
    li = aim_ref[...]
    step = jnp.exp(ls_ref[...])
    mag = jnp.exp(lr * step)
    ar = mag * jnp.cos(li * step)
    ai = mag * jnp.sin(li * step)
    xr = ar - 1.0
    den = lr * lr + li * li
    cr = (xr * lr + ai * li) / den
    ci = (ai * lr - xr * li) / den
    br = bre_ref[...]
    bi = bim_ref[...]
    abr_ref[...] = ar
    abi_ref[...] = ai
    bbr_ref[...] = cr * br - ci * bi
    bbi_ref[...] = cr * bi + ci * br


def _ssm_prep(a_re, a_im, log_step, b_re, b_im):
    g, p, c = b_re.shape
    rep = lambda a: jnp.repeat(a.astype(F32), c, axis=1)
    ls = jnp.broadcast_to(log_step.astype(F32)[:, None], (g, p * c))
    shp = jax.ShapeDtypeStruct((g, p * c), F32)
    abr, abi, bbr, bbi = pl.pallas_call(
        _ssm_prep_kernel, out_shape=(shp, shp, shp, shp), name="ssm_prep",
    )(rep(a_re), rep(a_im), ls, b_re.astype(F32).reshape(g, p * c), b_im.astype(F32).reshape(g, p * c))
    a_bar_re = abr.reshape(g, p, c)[:, :, 0]
    a_bar_im = abi.reshape(g, p, c)[:, :, 0]
    return a_bar_re, a_bar_im, bbr.reshape(g, p, c), bbi.reshape(g, p, c)


def _ssm_matrices(a_re, a_im, bb_re, bb_im, c_re, c_im):
    hi_p = lax.Precision.HIGHEST
    cr = c_re.astype(F32)
    ci = c_im.astype(F32)
    ar = a_re[:, :, None]
    ai = a_im[:, :, None]
    abr = ar * bb_re - ai * bb_im
    abi = ar * bb_im + ai * bb_re
    a2r = a_re * a_re - a_im * a_im
    a2i = 2.0 * a_re * a_im
    car = cr * a_re[:, None, :] - ci * a_im[:, None, :]
    cai = cr * a_im[:, None, :] + ci * a_re[:, None, :]
    ca2r = cr * a2r[:, None, :] - ci * a2i[:, None, :]
    ca2i = cr * a2i[:, None, :] + ci * a2r[:, None, :]
    k0 = jnp.einsum("gcp,gpd->gcd", cr, bb_re, precision=hi_p) - jnp.einsum("gcp,gpd->gcd", ci, bb_im, precision=hi_p)
    k1 = jnp.einsum("gcp,gpd->gcd", car, bb_re, precision=hi_p) - jnp.einsum("gcp,gpd->gcd", cai, bb_im, precision=hi_p)

    diag = jnp.eye(GROUPS_PER_OCTET, dtype=bool)[None, None, :, None, None, :, None]
    oct_shape = (N_OCTETS, GROUPS_PER_OCTET)
    zero = jnp.zeros_like(k0)
    t = jnp.stack([jnp.stack([k0, k1]), jnp.stack([zero, k0])]).reshape(2, 2, *oct_shape, SSM_GROUP, SSM_GROUP)
    t = t.transpose(2, 0, 5, 1, 3, 4)
    w_direct = jnp.where(diag, t[:, :, None], 0.0).reshape(N_OCTETS, 2 * LANES, 2 * LANES)
    s = jnp.stack([jnp.stack([abr, abi]), jnp.stack([bb_re, bb_im])]).reshape(2, 2, *oct_shape, SSM_STATE, SSM_GROUP)
    s = s.transpose(2, 0, 5, 1, 3, 4)
    w_state = jnp.where(diag, s[:, :, None], 0.0).reshape(N_OCTETS, 2 * LANES, 2 * OCTET_STATE)
    m = jnp.stack([jnp.stack([car, -cai]), jnp.stack([ca2r, -ca2i])]).reshape(2, 2, *oct_shape, SSM_GROUP, SSM_STATE)
    m = m.transpose(2, 1, 3, 5, 0, 4)
    w_carry = jnp.where(diag, m[:, :, :, :, :, None, :], 0.0).reshape(N_OCTETS, 2 * OCTET_STATE, 2 * LANES)
    a2r8 = jnp.broadcast_to(a2r.reshape(1, -1), (BATCH, SSM_GROUPS * SSM_STATE))
    a2i8 = jnp.broadcast_to(a2i.reshape(1, -1), (BATCH, SSM_GROUPS * SSM_STATE))
    return w_direct.astype(BF16), w_state.astype(BF16), w_carry.astype(BF16), a2r8, a2i8


def _ssm_layer_kernel(x_hbm, nrm_ref, win_ref, wd_ref, ws_ref, wc_ref, a2r_ref, a2i_ref, d_ref, wglu_ref, bglu_ref,
                      wout_ref, o_hbm, xin_scr, xout_scr, in_sem, out_sem, h_scr, proj_scr, hb_scr, y_scr,
                      *, t_block, n_steps):
    i = pl.program_id(0)
    slot = i % 2
    tm = t_block * BATCH
    n_chunks = t_block // 2
    cr = n_chunks * BATCH

    def in_copy(step, buf, b):
        return pltpu.make_async_copy(x_hbm.at[b, pl.ds(step * t_block, t_block), :], xin_scr.at[buf, :, b, :],
                                     in_sem.at[buf, b])

    def out_copy(step, buf, b):
        return pltpu.make_async_copy(xout_scr.at[buf, :, b, :], o_hbm.at[b, pl.ds(step * t_block, t_block), :],
                                     out_sem.at[buf, b])

    @pl.when(i == 0)
    def _():
        h_scr[...] = jnp.zeros_like(h_scr)
        for b in range(BATCH):
            in_copy(0, 0, b).start()

    @pl.when(i + 1 < n_steps)
    def _():
        for b in range(BATCH):
            in_copy(i + 1, 1 - slot, b).start()

    for b in range(BATCH):
        in_copy(i, slot, b).wait()

    x = xin_scr[slot].reshape(tm, D_MODEL)
    hn = _rms_scale(x, nrm_ref[...]).astype(BF16)
    proj_scr[...] = jnp.dot(hn, win_ref[...], preferred_element_type=F32).reshape(n_chunks, 2, BATCH, 2 * BRANCH)

    for o in range(N_OCTETS):
        ch = slice(o * LANES, (o + 1) * LANES)
        st = slice(o * 2 * OCTET_STATE, (o + 1) * 2 * OCTET_STATE)
        hb = hb_scr.at[o % 2]
        up = jnp.concatenate([proj_scr[:, 0, :, ch].reshape(cr, LANES), proj_scr[:, 1, :, ch].reshape(cr, LANES)],
                             axis=1).astype(BF16)
        yp = jnp.dot(up, wd_ref[o], preferred_element_type=F32)
        hb[BATCH:, :] = jnp.dot(up, ws_ref[o], preferred_element_type=F32)
        hb[:BATCH, :] = h_scr[:, st]
        a2r = a2r_ref[:, o * OCTET_STATE:(o + 1) * OCTET_STATE]
        a2i = a2i_ref[:, o * OCTET_STATE:(o + 1) * OCTET_STATE]
        hr = hb[:BATCH, :OCTET_STATE]
        hi = hb[:BATCH, OCTET_STATE:]
        for k in range(n_chunks):
            rows = slice((k + 1) * BATCH, (k + 2) * BATCH)
            hr, hi = (a2r * hr - a2i * hi + hb[rows, :OCTET_STATE], a2i * hr + a2r * hi + hb[rows, OCTET_STATE:])
            hb[rows, :OCTET_STATE] = hr
            hb[rows, OCTET_STATE:] = hi
        h_scr[:, st] = hb[cr:, :]
        yp = yp + jnp.dot(hb[:cr, :].astype(BF16), wc_ref[o], preferred_element_type=F32)
        y_scr[:, 0, :, ch] = yp[:, :LANES].reshape(n_chunks, BATCH, LANES)
        y_scr[:, 1, :, ch] = yp[:, LANES:].reshape(n_chunks, BATCH, LANES)

    u = proj_scr[:, :, :, :BRANCH].reshape(tm, BRANCH)
    z = _gelu_tanh(y_scr[...].reshape(tm, BRANCH) + d_ref[...] * u)
    glu = jnp.dot(z.astype(BF16), wglu_ref[...], preferred_element_type=F32) + bglu_ref[...]
    z = z * _sigmoid(glu)
    gate = proj_scr[:, :, :, BRANCH:].reshape(tm, BRANCH)
    act = (z * (gate * _sigmoid(gate))).astype(BF16)
    out = x + jnp.dot(act, wout_ref[...], preferred_element_type=F32)

    @pl.when(i >= 2)
    def _():
        for b in range(BATCH):
            out_copy(i - 2, slot, b).wait()

    xout_scr[slot] = out.reshape(t_block, BATCH, D_MODEL)
    for b in range(BATCH):
        out_copy(i, slot, b).start()

    @pl.when(i == n_steps - 1)
    def _():
        if n_steps >= 2:
            for b in range(BATCH):
                out_copy(i - 1, 1 - slot, b).wait()
        for b in range(BATCH):
            out_copy(i, slot, b).wait()


def _ssm_layer(x_bt, norm, w_in, a_re, a_im, log_step, b_re, b_im, c_re, c_im, d, w_glu, b_glu, w_out):
    bsz, seq, dm = x_bt.shape
    t_block = min(SSM_T_BLOCK, seq)
    n_steps = seq // t_block
    n_chunks = t_block // 2
    a_bar_re, a_bar_im, bb_re, bb_im = _ssm_prep(a_re, a_im, log_step, b_re, b_im)
    wd, ws, wc, a2r8, a2i8 = _ssm_matrices(a_bar_re, a_bar_im, bb_re, bb_im, c_re, c_im)
    n_state = SSM_GROUPS * SSM_STATE
    return pl.pallas_call(
        functools.partial(_ssm_layer_kernel, t_block=t_block, n_steps=n_steps),
        out_shape=jax.ShapeDtypeStruct((bsz, seq, dm), F32),
        grid=(n_steps,),
        in_specs=[
            pl.BlockSpec(memory_space=pl.ANY),
            _const_spec((1, dm)),
            _const_spec((dm, 2 * BRANCH)),
            _const_spec(wd.shape),
            _const_spec(ws.shape),
            _const_spec(wc.shape),
            _const_spec((BATCH, n_state)),
            _const_spec((BATCH, n_state)),
            _const_spec((1, BRANCH)),
            _const_spec((BRANCH, BRANCH)),
            _const_spec((1, BRANCH)),
            _const_spec((BRANCH, dm)),
        ],
        out_specs=pl.BlockSpec(memory_space=pl.ANY),
        scratch_shapes=[
            pltpu.VMEM((2, t_block, BATCH, dm), F32),
            pltpu.VMEM((2, t_block, BATCH, dm), F32),
            pltpu.SemaphoreType.DMA((2, BATCH)),
            pltpu.SemaphoreType.DMA((2, BATCH)),
            pltpu.VMEM((BATCH, 2 * n_state), F32),
            pltpu.VMEM((n_chunks, 2, BATCH, 2 * BRANCH), F32),
            pltpu.VMEM((2, (n_chunks + 1) * BATCH, 2 * OCTET_STATE), F32),
            pltpu.VMEM((n_chunks, 2, BATCH, BRANCH), F32),
        ],
        compiler_params=pltpu.CompilerParams(dimension_semantics=("arbitrary",), vmem_limit_bytes=VMEM_LIMIT),
        name="ssm_layer",
    )(x_bt, norm.astype(F32).reshape(1, dm), w_in.astype(BF16), wd, ws, wc, a2r8, a2i8,
      d.astype(F32).reshape(1, BRANCH), w_glu.astype(BF16), b_glu.astype(F32).reshape(1, BRANCH), w_out.astype(BF16))


def _rope_kernel(freq_ref, cos_ref, sin_ref):
    rows = cos_ref.shape[0]
    pos = (lax.broadcasted_iota(jnp.int32, cos_ref.shape, 0) + pl.program_id(0) * rows).astype(F32)
    ang = pos * freq_ref[...]
    lane = lax.broadcasted_iota(jnp.int32, cos_ref.shape, 1)
    sin = jnp.sin(ang)
    cos_ref[...] = jnp.cos(ang)
    sin_ref[...] = jnp.where(lane < LANES // 2, -sin, sin)


def _rope_tables(seq):
    inv_freq = (np.float32(ROPE_THETA) ** (-np.arange(0, HEAD_DIM, 2, dtype=np.float32) / np.float32(HEAD_DIM))).astype(np.float32)
    freq = jnp.asarray(np.tile(inv_freq, LANES // HALF).reshape(1, LANES))
    blk = min(1024, seq)
    spec = pl.BlockSpec((blk, LANES), lambda i: (i, 0))
    shp = jax.ShapeDtypeStruct((seq, LANES), F32)
    return pl.pallas_call(
        _rope_kernel, out_shape=(shp, shp), grid=(seq // blk,),
        in_specs=[pl.BlockSpec((1, LANES), lambda i: (0, 0))], out_specs=(spec, spec), name="rope_tables",
    )(freq)


def _attn_weight_layout(w_in):
    wq = w_in[:, :BRANCH] * (HEAD_DIM ** -0.5)
    wk = w_in[:, BRANCH:BRANCH + N_KV_HEADS * HEAD_DIM]
    wv = w_in[:, BRANCH + N_KV_HEADS * HEAD_DIM:BRANCH + 2 * N_KV_HEADS * HEAD_DIM]
    wg = w_in[:, BRANCH + 2 * N_KV_HEADS * HEAD_DIM:]
    dm = w_in.shape[0]
    wq = wq.reshape(dm, N_Q_HEADS // 2, 2, 2, HALF).transpose(0, 1, 3, 2, 4).reshape(dm, BRANCH)
    wk = jnp.broadcast_to(wk.reshape(dm, N_KV_HEADS, 2, 1, HALF), (dm, N_KV_HEADS, 2, 2, HALF))
    wk = wk.reshape(dm, N_KV_HEADS * LANES)
    wv = jnp.broadcast_to(wv.reshape(dm, N_KV_HEADS, 1, HEAD_DIM), (dm, N_KV_HEADS, 2, HEAD_DIM))
    wv = wv.reshape(dm, N_KV_HEADS * LANES)
    return jnp.concatenate([wq, wk, wv, wg], axis=1).astype(BF16)


_Q0 = 0
_K0 = BRANCH
_V0 = BRANCH + N_KV_HEADS * LANES
_G0 = BRANCH + 2 * N_KV_HEADS * LANES
_PROJ_COLS = _G0 + BRANCH
PAIRS_PER_KV = N_Q_HEADS // 2 // N_KV_HEADS


def _attn_layer_kernel(sinks_ref, x_ref, nrm_ref, win_ref, cos_ref, sin_ref, wout_ref, fin_ref, o_ref,
                       proj_scr, q_scr, kz_scr, vz_scr, o_scr, bias_scr, s_scr, mx_scr, es_scr, *, rows, final_norm):
    ti = pl.program_id(1)
    n_blk = rows // ATTN_BLOCK
    n_units = N_KV_HEADS * n_blk
    stack = PAIRS_PER_KV * ATTN_BLOCK
    window = 2 * ATTN_BLOCK

    @pl.when((pl.program_id(0) == 0) & (ti == 0))
    def _():
        qi = lax.broadcasted_iota(jnp.int32, (stack, window), 0) % ATTN_BLOCK
        kj = lax.broadcasted_iota(jnp.int32, (stack, window), 1)
        dist = qi + ATTN_BLOCK - kj
        band = (dist >= 0) & (dist < ATTN_BLOCK)
        bias_scr[0] = jnp.where(band, 0.0, NEG_INF)
        bias_scr[1] = jnp.where(band & (kj >= ATTN_BLOCK), 0.0, NEG_INF)

    @pl.when(ti == 0)
    def _():
        kz_scr[:, :, :ATTN_BLOCK, :] = jnp.zeros((N_KV_HEADS, 2, ATTN_BLOCK, LANES), BF16)
        vz_scr[:, :, :ATTN_BLOCK, :] = jnp.zeros((N_KV_HEADS, 2, ATTN_BLOCK, LANES), BF16)

    @pl.when(ti > 0)
    def _():
        kz_scr[:, :, :ATTN_BLOCK, :] = kz_scr[:, :, rows:, :]
        vz_scr[:, :, :ATTN_BLOCK, :] = vz_scr[:, :, rows:, :]

    x = x_ref[...]
    hn = _rms_scale(x, nrm_ref[...]).astype(BF16)
    proj_scr[...] = jnp.dot(hn, win_ref[...], preferred_element_type=F32)

    cos = cos_ref[...]
    sin = sin_ref[...]

    def rope(t):
        return t * cos + pltpu.roll(t, LANES // 2, axis=1) * sin

    for r in range(N_Q_HEADS // 2):
        q_scr[r] = rope(proj_scr[:, _Q0 + r * LANES:_Q0 + (r + 1) * LANES]).astype(BF16)

    lane = lax.broadcasted_iota(jnp.int32, (1, LANES), 1)
    for kv in range(N_KV_HEADS):
        k_rep = rope(proj_scr[:, _K0 + kv * LANES:_K0 + (kv + 1) * LANES])
        v_rep = proj_scr[:, _V0 + kv * LANES:_V0 + (kv + 1) * LANES]
        for slot in range(2):
            k_mask = (lane // HALF) % 2 == slot
            v_mask = lane // HEAD_DIM == slot
            kz_scr[kv, slot, ATTN_BLOCK:, :] = jnp.where(k_mask, k_rep, 0.0).astype(BF16)
            vz_scr[kv, slot, ATTN_BLOCK:, :] = jnp.where(v_mask, v_rep, 1.0).astype(BF16)

    def scores(u):
        kv, i = divmod(u, n_blk)
        row0 = i * ATTN_BLOCK
        first = jnp.logical_and(ti == 0, i == 0).astype(jnp.int32)
        q_stack = jnp.concatenate(
            [q_scr[kv * PAIRS_PER_KV + n, pl.ds(row0, ATTN_BLOCK), :] for n in range(PAIRS_PER_KV)], axis=0)
        for slot in range(2):
            sink = jnp.concatenate(
                [jnp.full((ATTN_BLOCK, LANES), sinks_ref[2 * (kv * PAIRS_PER_KV + n) + slot], F32)
                 for n in range(PAIRS_PER_KV)], axis=0)
            s = lax.dot_general(q_stack, kz_scr[kv, slot, pl.ds(row0, window), :], (((1,), (1,)), ((), ())),
                                preferred_element_type=F32) + bias_scr[first]
            mx = jnp.maximum(jnp.broadcast_to(jnp.max(s, axis=-1, keepdims=True), (stack, LANES)), sink)
            s_scr[u % 2, slot] = s
            mx_scr[u % 2, slot] = mx
            es_scr[u % 2, slot] = sink - mx

    def values(u):
        kv, i = divmod(u, n_blk)
        row0 = i * ATTN_BLOCK
        outs = []
        for slot in range(2):
            mx = mx_scr[u % 2, slot]
            p = jnp.exp(s_scr[u % 2, slot] - jnp.concatenate([mx, mx], axis=1)).astype(BF16)
            outs.append(jnp.dot(p, vz_scr[kv, slot, pl.ds(row0, window), :], preferred_element_type=F32))
        low = lax.broadcasted_iota(jnp.int32, (1, LANES), 1) < HEAD_DIM
        num = jnp.where(low, outs[0], outs[1])
        sums = pltpu.roll(jnp.where(low, outs[1], outs[0]), HEAD_DIM, axis=1)
        den = sums + jnp.exp(jnp.where(low, es_scr[u % 2, 0], es_scr[u % 2, 1]))
        o = num / den
        for n in range(PAIRS_PER_KV):
            o_scr[kv * PAIRS_PER_KV + n, pl.ds(row0, ATTN_BLOCK), :] = o[n * ATTN_BLOCK:(n + 1) * ATTN_BLOCK]

    scores(0)
    for u in range(1, n_units):
        values(u - 1)
        scores(u)
    values(n_units - 1)

    gate = proj_scr[:, _G0:]
    attn = jnp.concatenate([o_scr[r] for r in range(N_Q_HEADS // 2)], axis=1)
    act = (attn * (gate * _sigmoid(gate))).astype(BF16)
    y = x + jnp.dot(act, wout_ref[...], preferred_element_type=F32)
    if final_norm:
        y = _rms_scale(y, fin_ref[...])
    o_ref[...] = y


def _attn_layer(x_bt, norm, w_in, sinks, w_out, cos, sin, fin, *, final_norm):
    bsz, seq, dm = x_bt.shape
    rows = min(ATTN_ROWS, seq)
    stack = PAIRS_PER_KV * ATTN_BLOCK
    win = _attn_weight_layout(w_in.astype(F32))
    row_spec = pl.BlockSpec((None, rows, dm), lambda b, t: (b, t, 0))
    tab_spec = pl.BlockSpec((rows, LANES), lambda b, t: (t, 0))
    return pl.pallas_call(
        functools.partial(_attn_layer_kernel, rows=rows, final_norm=final_norm),
        out_shape=jax.ShapeDtypeStruct((bsz, seq, dm), F32),
        grid=(bsz, seq // rows),
        in_specs=[
            pl.BlockSpec(memory_space=pltpu.SMEM),
            row_spec,
            _const_spec((1, dm)),
            _const_spec((dm, _PROJ_COLS)),
            tab_spec,
            tab_spec,
            _const_spec((BRANCH, dm)),
            _const_spec((1, dm)),
        ],
        out_specs=row_spec,
        scratch_shapes=[
            pltpu.VMEM((rows, _PROJ_COLS), F32),
            pltpu.VMEM((N_Q_HEADS // 2, rows, LANES), BF16),
            pltpu.VMEM((N_KV_HEADS, 2, ATTN_BLOCK + rows, LANES), BF16),
            pltpu.VMEM((N_KV_HEADS, 2, ATTN_BLOCK + rows, LANES), BF16),
            pltpu.VMEM((N_Q_HEADS // 2, rows, LANES), F32),
            pltpu.VMEM((2, stack, 2 * ATTN_BLOCK), F32),
            pltpu.VMEM((2, 2, stack, 2 * ATTN_BLOCK), F32),
            pltpu.VMEM((2, 2, stack, LANES), F32),
            pltpu.VMEM((2, 2, stack, LANES), F32),
        ],
        compiler_params=pltpu.CompilerParams(dimension_semantics=("arbitrary", "arbitrary"), vmem_limit_bytes=VMEM_LIMIT),
        name="attn_layer",
    )(sinks.astype(F32), x_bt, norm.astype(F32).reshape(1, dm), win, cos, sin, w_out.astype(BF16),
      fin.astype(F32).reshape(1, dm))


def kernel(x, l0_norm, l0_w_in, l0_a_re, l0_a_im, l0_log_step, l0_b_re, l0_b_im, l0_c_re, l0_c_im, l0_d, l0_w_glu, l0_b_glu, l0_w_out, l1_norm, l1_w_in, l1_sinks, l1_w_out, l2_norm, l2_w_in, l2_a_re, l2_a_im, l2_log_step, l2_b_re, l2_b_im, l2_c_re, l2_c_im, l2_d, l2_w_glu, l2_b_glu, l2_w_out, l3_norm, l3_w_in, l3_sinks, l3_w_out, final_norm):
    bsz, seq, _ = x.shape
    assert bsz == BATCH
    cos, sin = _rope_tables(seq)
    h = x.astype(F32)
    h = _ssm_layer(h, l0_norm, l0_w_in, l0_a_re, l0_a_im, l0_log_step, l0_b_re, l0_b_im,
                   l0_c_re, l0_c_im, l0_d, l0_w_glu, l0_b_glu, l0_w_out)
    h = _attn_layer(h, l1_norm, l1_w_in, l1_sinks, l1_w_out, cos, sin, final_norm, final_norm=False)
    h = _ssm_layer(h, l2_norm, l2_w_in, l2_a_re, l2_a_im, l2_log_step, l2_b_re, l2_b_im,
                   l2_c_re, l2_c_im, l2_d, l2_w_glu, l2_b_glu, l2_w_out)
    h = _attn_layer(h, l3_norm, l3_w_in, l3_sinks, l3_w_out, cos, sin, final_norm, final_norm=True)
    return h.astype(x.dtype)
```

```python
import functools
import math

import numpy as np
import jax
import jax.numpy as jnp
from jax import lax
from jax.experimental import pallas as pl
from jax.experimental.pallas import tpu as pltpu

D_MODEL = 1024
BATCH = 8
BRANCH = D_MODEL
SSM_GROUP = 16
SSM_GROUPS = BRANCH // SSM_GROUP
SSM_STATE = 64
HEAD_DIM = 64
N_Q_HEADS = BRANCH // HEAD_DIM
N_KV_HEADS = 2
ATTN_BLOCK = 128
ROPE_THETA = 10000.0
NORM_EPS = 1e-5
NEG_INF = -1e30

LANES = 128
SSM_CHUNK = 4
QUAD_CH = LANES // 2
QUAD_GROUPS = QUAD_CH // SSM_GROUP
N_QUADS = BRANCH // QUAD_CH
QUAD_STATE = QUAD_GROUPS * SSM_STATE
HALF = HEAD_DIM // 2

SSM_T_BLOCK = 128
ATTN_ROWS = 512
VMEM_LIMIT = 56 * 1024 * 1024

F32 = jnp.float32
BF16 = jnp.bfloat16


def _sigmoid(v):
    return 1.0 / (1.0 + jnp.exp(-v))


def _gelu_tanh(v):
    c = math.sqrt(2.0 / math.pi)
    return 0.5 * v * (1.0 + jnp.tanh(c * (v + 0.044715 * (v * v * v))))


def _rms_scale(x, g):
    ms = jnp.mean(x * x, axis=-1, keepdims=True)
    return x * lax.rsqrt(ms + NORM_EPS) * g


def _const_spec(shape):
    nd = len(shape)
    return pl.BlockSpec(shape, lambda *_: (0,) * nd, pipeline_mode=pl.Buffered(1))


def _ssm_prep_kernel(are_ref, aim_ref, ls_ref, bre_ref, bim_ref, abr_ref, abi_ref, bbr_ref, bbi_ref):
    lr = are_ref[...]
    li = aim_ref[...]
    step = jnp.exp(ls_ref[...])
    mag = jnp.exp(lr * step)
    ar = mag * jnp.cos(li * step)
    ai = mag * jnp.sin(li * step)
    xr = ar - 1.0
    den = lr * lr + li * li
    cr = (xr * lr + ai * li) / den
    ci = (ai * lr - xr * li) / den
    br = bre_ref[...]
    bi = bim_ref[...]
    abr_ref[...] = ar
    abi_ref[...] = ai
    bbr_ref[...] = cr * br - ci * bi
    bbi_ref[...] = cr * bi + ci * br


def _ssm_prep(a_re, a_im, log_step, b_re, b_im):
    g, p, c = b_re.shape
    rep = lambda a: jnp.repeat(a.astype(F32), c, axis=1)
    ls = jnp.broadcast_to(log_step.astype(F32)[:, None], (g, p * c))
    shp = jax.ShapeDtypeStruct((g, p * c), F32)
    abr, abi, bbr, bbi = pl.pallas_call(
        _ssm_prep_kernel, out_shape=(shp, shp, shp, shp), name="ssm_prep",
    )(rep(a_re), rep(a_im), ls, b_re.astype(F32).reshape(g, p * c), b_im.astype(F32).reshape(g, p * c))
    a_bar_re = abr.reshape(g, p, c)[:, :, 0]
    a_bar_im = abi.reshape(g, p, c)[:, :, 0]
    return a_bar_re, a_bar_im, bbr.reshape(g, p, c), bbi.reshape(g, p, c)


def _cmul(ar, ai, br, bi):
    return ar * br - ai * bi, ar * bi + ai * br


def _ssm_matrices(a_re, a_im, bb_re, bb_im, c_re, c_im):
    hi_p = lax.Precision.HIGHEST
    r = SSM_CHUNK
    cr = c_re.astype(F32)
    ci = c_im.astype(F32)
    pw = [(jnp.ones_like(a_re), jnp.zeros_like(a_im))]
    for _ in range(r):
        pw.append(_cmul(pw[-1][0], pw[-1][1], a_re, a_im))
    ca = [_cmul(cr, ci, p_r[:, None, :], p_i[:, None, :]) for p_r, p_i in pw]
    ab = [_cmul(p_r[:, :, None], p_i[:, :, None], bb_re, bb_im) for p_r, p_i in pw[:r]]
    kd = [jnp.einsum("gcp,gpd->gcd", ca[k][0], bb_re, precision=hi_p)
          - jnp.einsum("gcp,gpd->gcd", ca[k][1], bb_im, precision=hi_p) for k in range(r)]

    diag = jnp.eye(QUAD_GROUPS, dtype=bool)[None, None, :, None, None, :, None]
    n_quads = a_re.shape[0] // QUAD_GROUPS
    quad_shape = (n_quads, QUAD_GROUPS)
    zero = jnp.zeros_like(kd[0])
    t = jnp.stack([jnp.stack([kd[d - e] if d >= e else zero for d in range(r)]) for e in range(r)])
    t = t.reshape(r, r, *quad_shape, SSM_GROUP, SSM_GROUP).transpose(2, 0, 5, 1, 3, 4)
    w_direct = jnp.where(diag, t[:, :, None], 0.0).reshape(n_quads, r * QUAD_CH, r * QUAD_CH)
    s = jnp.stack([jnp.stack(ab[r - 1 - e]) for e in range(r)])
    s = s.reshape(r, 2, *quad_shape, SSM_STATE, SSM_GROUP).transpose(2, 0, 5, 1, 3, 4)
    w_state = jnp.where(diag, s[:, :, None], 0.0).reshape(n_quads, r * QUAD_CH, 2 * QUAD_STATE)
    m = jnp.stack([jnp.stack([ca[d + 1][0], -ca[d + 1][1]]) for d in range(r)])
    m = m.reshape(r, 2, *quad_shape, SSM_GROUP, SSM_STATE).transpose(2, 1, 3, 5, 0, 4)
    w_carry = jnp.where(diag, m[:, :, :, :, :, None, :], 0.0).reshape(n_quads, 2 * QUAD_STATE, r * QUAD_CH)
    arr8 = jnp.broadcast_to(pw[r][0].reshape(1, -1), (BATCH, a_re.size))
    ari8 = jnp.broadcast_to(pw[r][1].reshape(1, -1), (BATCH, a_re.size))
    return w_direct.astype(BF16), w_state.astype(BF16), w_carry.astype(BF16), arr8, ari8


def _ssm_weights(layers):
    cat = [jnp.concatenate([lay[k].astype(F32) for lay in layers], axis=0) for k in range(7)]
    a_bar_re, a_bar_im, bb_re, bb_im = _ssm_prep(*cat[:5])
    wd, ws, wc, arr8, ari8 = _ssm_matrices(a_bar_re, a_bar_im, bb_re, bb_im, cat[5], cat[6])
    n_state = SSM_GROUPS * SSM_STATE
    return [(wd[n * N_QUADS:(n + 1) * N_QUADS], ws[n * N_QUADS:(n + 1) * N_QUADS], wc[n * N_QUADS:(n + 1) * N_QUADS],
             arr8[:, n * n_state:(n + 1) * n_state], ari8[:, n * n_state:(n + 1) * n_state])
            for n in range(len(layers))]


def _ssm_layer_kernel(x_hbm, nrm_ref, win_ref, wd_ref, ws_ref, wc_ref, a2r_ref, a2i_ref, d_ref, wglu_ref, bglu_ref,
                      wout_ref, o_hbm, xin_scr, xout_scr, in_sem, out_sem, h_scr, proj_scr, hb_scr, y_scr,
                      *, t_block, n_steps):
    i = pl.program_id(0)
    slot = i % 2
    tm = t_block * BATCH
    n_chunks = t_block // SSM_CHUNK
    cr = n_chunks * BATCH

    def in_copy(step, buf, b):
        return pltpu.make_async_copy(x_hbm.at[b, pl.ds(step * t_block, t_block), :], xin_scr.at[buf, :, b, :],
                                     in_sem.at[buf, b])

    def out_copy(step, buf, b):
        return pltpu.make_async_copy(xout_scr.at[buf, :, b, :], o_hbm.at[b, pl.ds(step * t_block, t_block), :],
                                     out_sem.at[buf, b])

    @pl.when(i == 0)
    def _():
        h_scr[...] = jnp.zeros_like(h_scr)
        for b in range(BATCH):
            in_copy(0, 0, b).start()

    @pl.when(i + 1 < n_steps)
    def _():
        for b in range(BATCH):
            in_copy(i + 1, 1 - slot, b).start()

    for b in range(BATCH):
        in_copy(i, slot, b).wait()

    x = xin_scr[slot].reshape(tm, D_MODEL)
    hn = _rms_scale(x, nrm_ref[...]).astype(BF16)
    proj_scr[...] = jnp.dot(hn, win_ref[...], preferred_element_type=F32).reshape(n_chunks, SSM_CHUNK, BATCH, 2 * BRANCH)

    low = lax.broadcasted_iota(jnp.int32, (1, LANES), 1) < QUAD_CH
    for m in range(BRANCH // LANES):
        ch = slice(m * LANES, (m + 1) * LANES)
        src = [proj_scr[:, e, :, ch].reshape(cr, LANES) for e in range(SSM_CHUNK)]
        swp = [pltpu.roll(v, QUAD_CH, axis=1) for v in src]
        ups = (jnp.concatenate([jnp.where(low, src[0], swp[1]), jnp.where(low, src[2], swp[3])], axis=1),
               jnp.concatenate([jnp.where(low, swp[0], src[1]), jnp.where(low, swp[2], src[3])], axis=1))
        yps = []
        for half, up in enumerate(ups):
            q = 2 * m + half
            st = slice(q * 2 * QUAD_STATE, (q + 1) * 2 * QUAD_STATE)
            hb = hb_scr.at[m % 2, half]
            up = up.astype(BF16)
            yp = jnp.dot(up, wd_ref[q], preferred_element_type=F32)
            hb[BATCH:, :] = jnp.dot(up, ws_ref[q], preferred_element_type=F32)
            hb[:BATCH, :] = h_scr[:, st]
            arr = a2r_ref[:, q * QUAD_STATE:(q + 1) * QUAD_STATE]
            ari = a2i_ref[:, q * QUAD_STATE:(q + 1) * QUAD_STATE]
            hr = hb[:BATCH, :QUAD_STATE]
            hi = hb[:BATCH, QUAD_STATE:]
            for k in range(n_chunks):
                rows = slice((k + 1) * BATCH, (k + 2) * BATCH)
                hr, hi = (arr * hr - ari * hi + hb[rows, :QUAD_STATE], ari * hr + arr * hi + hb[rows, QUAD_STATE:])
                hb[rows, :QUAD_STATE] = hr
                hb[rows, QUAD_STATE:] = hi
            h_scr[:, st] = hb[cr:, :]
            yps.append(yp + jnp.dot(hb[:cr, :].astype(BF16), wc_ref[q], preferred_element_type=F32))
        for n in range(SSM_CHUNK // 2):
            y0 = yps[0][:, n * LANES:(n + 1) * LANES]
            y1 = yps[1][:, n * LANES:(n + 1) * LANES]
            y_scr[:, 2 * n, :, ch] = jnp.where(low, y0, pltpu.roll(y1, QUAD_CH, axis=1)).reshape(n_chunks, BATCH, LANES)
            y_scr[:, 2 * n + 1, :, ch] = jnp.where(low, pltpu.roll(y0, QUAD_CH, axis=1), y1).reshape(n_chunks, BATCH, LANES)

    u = proj_scr[:, :, :, :BRANCH].reshape(tm, BRANCH)
    z = _gelu_tanh(y_scr[...].reshape(tm, BRANCH) + d_ref[...] * u)
    glu = jnp.dot(z.astype(BF16), wglu_ref[...], preferred_element_type=F32) + bglu_ref[...]
    z = z * _sigmoid(glu)
    gate = proj_scr[:, :, :, BRANCH:].reshape(tm, BRANCH)
    act = (z * (gate * _sigmoid(gate))).astype(BF16)
    out = x + jnp.dot(act, wout_ref[...], preferred_element_type=F32)

    @pl.when(i >= 2)
    def _():
        for b in range(BATCH):
            out_copy(i - 2, slot, b).wait()

    xout_scr[slot] = out.reshape(t_block, BATCH, D_MODEL)
    for b in range(BATCH):
        out_copy(i, slot, b).start()

    @pl.when(i == n_steps - 1)
    def _():
        if n_steps >= 2:
            for b in range(BATCH):
                out_copy(i - 1, 1 - slot, b).wait()
        for b in range(BATCH):
            out_copy(i, slot, b).wait()


def _ssm_layer(x_bt, norm, w_in, mixer, d, w_glu, b_glu, w_out):
    bsz, seq, dm = x_bt.shape
    t_block = min(SSM_T_BLOCK, seq)
    n_steps = seq // t_block
    n_chunks = t_block // SSM_CHUNK
    wd, ws, wc, a2r8, a2i8 = mixer
    n_state = SSM_GROUPS * SSM_STATE
    return pl.pallas_call(
        functools.partial(_ssm_layer_kernel, t_block=t_block, n_steps=n_steps),
        out_shape=jax.ShapeDtypeStruct((bsz, seq, dm), F32),
        grid=(n_steps,),
        in_specs=[
            pl.BlockSpec(memory_space=pl.ANY),
            _const_spec((1, dm)),
            _const_spec((dm, 2 * BRANCH)),
            _const_spec(wd.shape),
            _const_spec(ws.shape),
            _const_spec(wc.shape),
            _const_spec((BATCH, n_state)),
            _const_spec((BATCH, n_state)),
            _const_spec((1, BRANCH)),
            _const_spec((BRANCH, BRANCH)),
            _const_spec((1, BRANCH)),
            _const_spec((BRANCH, dm)),
        ],
        out_specs=pl.BlockSpec(memory_space=pl.ANY),
        scratch_shapes=[
            pltpu.VMEM((2, t_block, BATCH, dm), F32),
            pltpu.VMEM((2, t_block, BATCH, dm), F32),
            pltpu.SemaphoreType.DMA((2, BATCH)),
            pltpu.SemaphoreType.DMA((2, BATCH)),
            pltpu.VMEM((BATCH, 2 * n_state), F32),
            pltpu.VMEM((n_chunks, SSM_CHUNK, BATCH, 2 * BRANCH), F32),
            pltpu.VMEM((2, 2, (n_chunks + 1) * BATCH, 2 * QUAD_STATE), F32),
            pltpu.VMEM((n_chunks, SSM_CHUNK, BATCH, BRANCH), F32),
        ],
        compiler_params=pltpu.CompilerParams(dimension_semantics=("arbitrary",), vmem_limit_bytes=VMEM_LIMIT),
        name="ssm_layer",
    )(x_bt, norm.astype(F32).reshape(1, dm), w_in.astype(BF16), wd, ws, wc, a2r8, a2i8,
      d.astype(F32).reshape(1, BRANCH), w_glu.astype(BF16), b_glu.astype(F32).reshape(1, BRANCH), w_out.astype(BF16))


def _rope_kernel(freq_ref, cos_ref, sin_ref):
    rows = cos_ref.shape[0]
    pos = (lax.broadcasted_iota(jnp.int32, cos_ref.shape, 0) + pl.program_id(0) * rows).astype(F32)
    ang = pos * freq_ref[...]
    lane = lax.broadcasted_iota(jnp.int32, cos_ref.shape, 1)
    sin = jnp.sin(ang)
    cos_ref[...] = jnp.cos(ang)
    sin_ref[...] = jnp.where(lane < LANES // 2, -sin, sin)


def _rope_tables(seq):
    inv_freq = (np.float32(ROPE_THETA) ** (-np.arange(0, HEAD_DIM, 2, dtype=np.float32) / np.float32(HEAD_DIM))).astype(np.float32)
    freq = jnp.asarray(np.tile(inv_freq, LANES // HALF).reshape(1, LANES))
    blk = min(1024, seq)
    spec = pl.BlockSpec((blk, LANES), lambda i: (i, 0))
    shp = jax.ShapeDtypeStruct((seq, LANES), F32)
    return pl.pallas_call(
        _rope_kernel, out_shape=(shp, shp), grid=(seq // blk,),
        in_specs=[pl.BlockSpec((1, LANES), lambda i: (0, 0))], out_specs=(spec, spec), name="rope_tables",
    )(freq)


def _attn_weight_layout(w_in):
    wq = w_in[:, :BRANCH] * (HEAD_DIM ** -0.5)
    wk = w_in[:, BRANCH:BRANCH + N_KV_HEADS * HEAD_DIM]
    wv = w_in[:, BRANCH + N_KV_HEADS * HEAD_DIM:BRANCH + 2 * N_KV_HEADS * HEAD_DIM]
    wg = w_in[:, BRANCH + 2 * N_KV_HEADS * HEAD_DIM:]
    dm = w_in.shape[0]
    wq = wq.reshape(dm, N_Q_HEADS // 2, 2, 2, HALF).transpose(0, 1, 3, 2, 4).reshape(dm, BRANCH)
    wk = jnp.broadcast_to(wk.reshape(dm, N_KV_HEADS, 2, 1, HALF), (dm, N_KV_HEADS, 2, 2, HALF))
    wk = wk.reshape(dm, N_KV_HEADS * LANES)
    wv = jnp.broadcast_to(wv.reshape(dm, N_KV_HEADS, 1, HEAD_DIM), (dm, N_KV_HEADS, 2, HEAD_DIM))
    wv = wv.reshape(dm, N_KV_HEADS * LANES)
    return jnp.concatenate([wq, wk, wv, wg], axis=1).astype(BF16)


_Q0 = 0
_K0 = BRANCH
_V0 = BRANCH + N_KV_HEADS * LANES
_G0 = BRANCH + 2 * N_KV_HEADS * LANES
_PROJ_COLS = _G0 + BRANCH
PAIRS_PER_KV = N_Q_HEADS // 2 // N_KV_HEADS


def _attn_layer_kernel(sinks_ref, x_ref, nrm_ref, win_ref, cos_ref, sin_ref, wout_ref, fin_ref, o_ref,
                       proj_scr, q_scr, kz_scr, vz_scr, o_scr, bias_scr, s_scr, mx_scr, es_scr, *, rows, final_norm):
    ti = pl.program_id(1)
    n_blk = rows // ATTN_BLOCK
    n_units = N_KV_HEADS * n_blk
    stack = PAIRS_PER_KV * ATTN_BLOCK
    window = 2 * ATTN_BLOCK

    @pl.when((pl.program_id(0) == 0) & (ti == 0))
    def _():
        qi = lax.broadcasted_iota(jnp.int32, (stack, window), 0) % ATTN_BLOCK
        kj = lax.broadcasted_iota(jnp.int32, (stack, window), 1)
        dist = qi + ATTN_BLOCK - kj
        band = (dist >= 0) & (dist < ATTN_BLOCK)
        bias_scr[0] = jnp.where(band, 0.0, NEG_INF)
        bias_scr[1] = jnp.where(band & (kj >= ATTN_BLOCK), 0.0, NEG_INF)

    @pl.when(ti == 0)
    def _():
        kz_scr[:, :, :ATTN_BLOCK, :] = jnp.zeros((N_KV_HEADS, 2, ATTN_BLOCK, LANES), BF16)
        vz_scr[:, :, :ATTN_BLOCK, :] = jnp.zeros((N_KV_HEADS, 2, ATTN_BLOCK, LANES), BF16)

    @pl.when(ti > 0)
    def _():
        kz_scr[:, :, :ATTN_BLOCK, :] = kz_scr[:, :, rows:, :]
        vz_scr[:, :, :ATTN_BLOCK, :] = vz_scr[:, :, rows:, :]

    x = x_ref[...]
    hn = _rms_scale(x, nrm_ref[...]).astype(BF16)
    proj_scr[...] = jnp.dot(hn, win_ref[...], preferred_element_type=F32)

    cos = cos_ref[...]
    sin = sin_ref[...]

    def rope(t):
        return t * cos + pltpu.roll(t, LANES // 2, axis=1) * sin

    for r in range(N_Q_HEADS // 2):
        q_scr[r] = rope(proj_scr[:, _Q0 + r * LANES:_Q0 + (r + 1) * LANES]).astype(BF16)

    lane = lax.broadcasted_iota(jnp.int32, (1, LANES), 1)
    for kv in range(N_KV_HEADS):
        k_rep = rope(proj_scr[:, _K0 + kv * LANES:_K0 + (kv + 1) * LANES])
        v_rep = proj_scr[:, _V0 + kv * LANES:_V0 + (kv + 1) * LANES]
        for slot in range(2):
            k_mask = (lane // HALF) % 2 == slot
            v_mask = lane // HEAD_DIM == slot
            kz_scr[kv, slot, ATTN_BLOCK:, :] = jnp.where(k_mask, k_rep, 0.0).astype(BF16)
            vz_scr[kv, slot, ATTN_BLOCK:, :] = jnp.where(v_mask, v_rep, 1.0).astype(BF16)

    def scores(u):
        kv, i = divmod(u, n_blk)
        row0 = i * ATTN_BLOCK
        first = jnp.logical_and(ti == 0, i == 0).astype(jnp.int32)
        q_stack = jnp.concatenate(
            [q_scr[kv * PAIRS_PER_KV + n, pl.ds(row0, ATTN_BLOCK), :] for n in range(PAIRS_PER_KV)], axis=0)
        for slot in range(2):
            sink = jnp.concatenate(
                [jnp.full((ATTN_BLOCK, LANES), sinks_ref[2 * (kv * PAIRS_PER_KV + n) + slot], F32)
                 for n in range(PAIRS_PER_KV)], axis=0)
            s = lax.dot_general(q_stack, kz_scr[kv, slot, pl.ds(row0, window), :], (((1,), (1,)), ((), ())),
                                preferred_element_type=F32) + bias_scr[first]
            mx = jnp.maximum(jnp.broadcast_to(jnp.max(s, axis=-1, keepdims=True), (stack, LANES)), sink)
            s_scr[u % 2, slot] = s
            mx_scr[u % 2, slot] = mx
            es_scr[u % 2, slot] = sink - mx

    def values(u):
        kv, i = divmod(u, n_blk)
        row0 = i * ATTN_BLOCK
        outs = []
        for slot in range(2):
            mx = mx_scr[u % 2, slot]
            p = jnp.exp(s_scr[u % 2, slot] - jnp.concatenate([mx, mx], axis=1)).astype(BF16)
            outs.append(jnp.dot(p, vz_scr[kv, slot, pl.ds(row0, window), :], preferred_element_type=F32))
        low = lax.broadcasted_iota(jnp.int32, (1, LANES), 1) < HEAD_DIM
        num = jnp.where(low, outs[0], outs[1])
        sums = pltpu.roll(jnp.where(low, outs[1], outs[0]), HEAD_DIM, axis=1)
        den = sums + jnp.exp(jnp.where(low, es_scr[u % 2, 0], es_scr[u % 2, 1]))
        o = num / den
        for n in range(PAIRS_PER_KV):
            o_scr[kv * PAIRS_PER_KV + n, pl.ds(row0, ATTN_BLOCK), :] = o[n * ATTN_BLOCK:(n + 1) * ATTN_BLOCK]

    scores(0)
    for u in range(1, n_units):
        values(u - 1)
        scores(u)
    values(n_units - 1)

    gate = proj_scr[:, _G0:]
    attn = jnp.concatenate([o_scr[r] for r in range(N_Q_HEADS // 2)], axis=1)
    act = (attn * (gate * _sigmoid(gate))).astype(BF16)
    y = x + jnp.dot(act, wout_ref[...], preferred_element_type=F32)
    if final_norm:
        y = _rms_scale(y, fin_ref[...])
    o_ref[...] = y


def _attn_layer(x_bt, norm, win, sinks, w_out, cos, sin, fin, *, final_norm):
    bsz, seq, dm = x_bt.shape
    rows = min(ATTN_ROWS, seq)
    stack = PAIRS_PER_KV * ATTN_BLOCK
    row_spec = pl.BlockSpec((None, rows, dm), lambda b, t: (b, t, 0))
    tab_spec = pl.BlockSpec((rows, LANES), lambda b, t: (t, 0))
    return pl.pallas_call(
        functools.partial(_attn_layer_kernel, rows=rows, final_norm=final_norm),
        out_shape=jax.ShapeDtypeStruct((bsz, seq, dm), F32),
        grid=(bsz, seq // rows),
        in_specs=[
            pl.BlockSpec(memory_space=pltpu.SMEM),
            row_spec,
            _const_spec((1, dm)),
            _const_spec((dm, _PROJ_COLS)),
            tab_spec,
            tab_spec,
            _const_spec((BRANCH, dm)),
            _const_spec((1, dm)),
        ],
        out_specs=row_spec,
        scratch_shapes=[
            pltpu.VMEM((rows, _PROJ_COLS), F32),
            pltpu.VMEM((N_Q_HEADS // 2, rows, LANES), BF16),
            pltpu.VMEM((N_KV_HEADS, 2, ATTN_BLOCK + rows, LANES), BF16),
            pltpu.VMEM((N_KV_HEADS, 2, ATTN_BLOCK + rows, LANES), BF16),
            pltpu.VMEM((N_Q_HEADS // 2, rows, LANES), F32),
            pltpu.VMEM((2, stack, 2 * ATTN_BLOCK), F32),
            pltpu.VMEM((2, 2, stack, 2 * ATTN_BLOCK), F32),
            pltpu.VMEM((2, 2, stack, LANES), F32),
            pltpu.VMEM((2, 2, stack, LANES), F32),
        ],
        compiler_params=pltpu.CompilerParams(dimension_semantics=("arbitrary", "arbitrary"), vmem_limit_bytes=VMEM_LIMIT),
        name="attn_layer",
    )(sinks.astype(F32), x_bt, norm.astype(F32).reshape(1, dm), win, cos, sin, w_out.astype(BF16),
      fin.astype(F32).reshape(1, dm))


def kernel(x, l0_norm, l0_w_in, l0_a_re, l0_a_im, l0_log_step, l0_b_re, l0_b_im, l0_c_re, l0_c_im, l0_d, l0_w_glu, l0_b_glu, l0_w_out, l1_norm, l1_w_in, l1_sinks, l1_w_out, l2_norm, l2_w_in, l2_a_re, l2_a_im, l2_log_step, l2_b_re, l2_b_im, l2_c_re, l2_c_im, l2_d, l2_w_glu, l2_b_glu, l2_w_out, l3_norm, l3_w_in, l3_sinks, l3_w_out, final_norm):
    bsz, seq, _ = x.shape
    assert bsz == BATCH
    cos, sin = _rope_tables(seq)
    mix0, mix2 = _ssm_weights([(l0_a_re, l0_a_im, l0_log_step, l0_b_re, l0_b_im, l0_c_re, l0_c_im),
                               (l2_a_re, l2_a_im, l2_log_step, l2_b_re, l2_b_im, l2_c_re, l2_c_im)])
    win1, win3 = jax.vmap(_attn_weight_layout)(jnp.stack([l1_w_in, l3_w_in]).astype(F32))
    h = x.astype(F32)
    h = _ssm_layer(h, l0_norm, l0_w_in, mix0, l0_d, l0_w_glu, l0_b_glu, l0_w_out)
    h = _attn_layer(h, l1_norm, win1, l1_sinks, l1_w_out, cos, sin, final_norm, final_norm=False)
    h = _ssm_layer(h, l2_norm, l2_w_in, mix2, l2_d, l2_w_glu, l2_b_glu, l2_w_out)
    h = _attn_layer(h, l3_norm, win3, l3_sinks, l3_w_out, cos, sin, final_norm, final_norm=True)
    return h.astype(x.dtype)
```

```python
import functools
import math

import numpy as np
import jax
import jax.numpy as jnp
from jax import lax
from jax.experimental import pallas as pl
from jax.experimental.pallas import tpu as pltpu

D_MODEL = 1024
BATCH = 8
BRANCH = D_MODEL
SSM_GROUP = 16
SSM_GROUPS = BRANCH // SSM_GROUP
SSM_STATE = 64
HEAD_DIM = 64
N_Q_HEADS = BRANCH // HEAD_DIM
N_KV_HEADS = 2
ATTN_BLOCK = 128
ROPE_THETA = 10000.0
NORM_EPS = 1e-5
NEG_INF = -1e30

LANES = 128
SSM_CHUNK = 4
QUAD_CH = LANES // 2
QUAD_GROUPS = QUAD_CH // SSM_GROUP
N_QUADS = BRANCH // QUAD_CH
QUAD_STATE = QUAD_GROUPS * SSM_STATE
HALF = HEAD_DIM // 2

SSM_T_BLOCK = 128
ATTN_ROWS = 512
VMEM_LIMIT = 56 * 1024 * 1024

F32 = jnp.float32
BF16 = jnp.bfloat16


def _sigmoid(v):
    return 1.0 / (1.0 + jnp.exp(-v))


def _gelu_tanh(v):
    c = math.sqrt(2.0 / math.pi)
    return 0.5 * v * (1.0 + jnp.tanh(c * (v + 0.044715 * (v * v * v))))


def _rms_scale(x, g):
    ms = jnp.mean(x * x, axis=-1, keepdims=True)
    return x * lax.rsqrt(ms + NORM_EPS) * g


def _const_spec(shape):
    nd = len(shape)
    return pl.BlockSpec(shape, lambda *_: (0,) * nd, pipeline_mode=pl.Buffered(1))


def _ssm_prep_kernel(are_ref, aim_ref, ls_ref, bre_ref, bim_ref, abr_ref, abi_ref, bbr_ref, bbi_ref):
    lr = are_ref[...]
    li = aim_ref[...]
    step = jnp.exp(ls_ref[...])
    mag = jnp.exp(lr * step)
    ar = mag * jnp.cos(li * step)
    ai = mag * jnp.sin(li * step)
    xr = ar - 1.0
    den = lr * lr + li * li
    cr = (xr * lr + ai * li) / den
    ci = (ai * lr - xr * li) / den
    br = bre_ref[...]
    bi = bim_ref[...]
    abr_ref[...] = ar
    abi_ref[...] = ai
    bbr_ref[...] = cr * br - ci * bi
    bbi_ref[...] = cr * bi + ci * br


def _ssm_prep(a_re, a_im, log_step, b_re, b_im):
    g, p, c = b_re.shape
    rep = lambda a: jnp.repeat(a.astype(F32), c, axis=1)
    ls = jnp.broadcast_to(log_step.astype(F32)[:, None], (g, p * c))
    shp = jax.ShapeDtypeStruct((g, p * c), F32)
    abr, abi, bbr, bbi = pl.pallas_call(
        _ssm_prep_kernel, out_shape=(shp, shp, shp, shp), name="ssm_prep",
    )(rep(a_re), rep(a_im), ls, b_re.astype(F32).reshape(g, p * c), b_im.astype(F32).reshape(g, p * c))
    a_bar_re = abr.reshape(g, p, c)[:, :, 0]
    a_bar_im = abi.reshape(g, p, c)[:, :, 0]
    return a_bar_re, a_bar_im, bbr.reshape(g, p, c), bbi.reshape(g, p, c)


def _cmul(ar, ai, br, bi):
    return ar * br - ai * bi, ar * bi + ai * br


def _ssm_matrices(a_re, a_im, bb_re, bb_im, c_re, c_im):
    hi_p = lax.Precision.HIGHEST
    r = SSM_CHUNK
    cr = c_re.astype(F32)
    ci = c_im.astype(F32)
    pw = [(jnp.ones_like(a_re), jnp.zeros_like(a_im))]
    for _ in range(r):
        pw.append(_cmul(pw[-1][0], pw[-1][1], a_re, a_im))
    ca = [_cmul(cr, ci, p_r[:, None, :], p_i[:, None, :]) for p_r, p_i in pw]
    ab = [_cmul(p_r[:, :, None], p_i[:, :, None], bb_re, bb_im) for p_r, p_i in pw[:r]]
    kd = [jnp.einsum("gcp,gpd->gcd", ca[k][0], bb_re, precision=hi_p)
          - jnp.einsum("gcp,gpd->gcd", ca[k][1], bb_im, precision=hi_p) for k in range(r)]

    diag = jnp.eye(QUAD_GROUPS, dtype=bool)[None, None, :, None, None, :, None]
    n_quads = a_re.shape[0] // QUAD_GROUPS
    quad_shape = (n_quads, QUAD_GROUPS)
    zero = jnp.zeros_like(kd[0])
    t = jnp.stack([jnp.stack([kd[d - e] if d >= e else zero for d in range(r)]) for e in range(r)])
    t = t.reshape(r, r, *quad_shape, SSM_GROUP, SSM_GROUP).transpose(2, 0, 5, 1, 3, 4)
    w_direct = jnp.where(diag, t[:, :, None], 0.0).reshape(n_quads, r * QUAD_CH, r * QUAD_CH)
    s = jnp.stack([jnp.stack(ab[r - 1 - e]) for e in range(r)])
    s = s.reshape(r, 2, *quad_shape, SSM_STATE, SSM_GROUP).transpose(2, 0, 5, 1, 3, 4)
    w_state = jnp.where(diag, s[:, :, None], 0.0).reshape(n_quads, r * QUAD_CH, 2 * QUAD_STATE)
    m = jnp.stack([jnp.stack([ca[d + 1][0], -ca[d + 1][1]]) for d in range(r)])
    m = m.reshape(r, 2, *quad_shape, SSM_GROUP, SSM_STATE).transpose(2, 1, 3, 5, 0, 4)
    w_carry = jnp.where(diag, m[:, :, :, :, :, None, :], 0.0).reshape(n_quads, 2 * QUAD_STATE, r * QUAD_CH)
    arr8 = jnp.broadcast_to(pw[r][0].reshape(1, -1), (BATCH, a_re.size))
    ari8 = jnp.broadcast_to(pw[r][1].reshape(1, -1), (BATCH, a_re.size))
    return w_direct.astype(BF16), w_state.astype(BF16), w_carry.astype(BF16), arr8, ari8


def _ssm_weights(layers):
    cat = [jnp.concatenate([lay[k].astype(F32) for lay in layers], axis=0) for k in range(7)]
    a_bar_re, a_bar_im, bb_re, bb_im = _ssm_prep(*cat[:5])
    wd, ws, wc, arr8, ari8 = _ssm_matrices(a_bar_re, a_bar_im, bb_re, bb_im, cat[5], cat[6])
    n_state = SSM_GROUPS * SSM_STATE
    return [(wd[n * N_QUADS:(n + 1) * N_QUADS], ws[n * N_QUADS:(n + 1) * N_QUADS], wc[n * N_QUADS:(n + 1) * N_QUADS],
             arr8[:, n * n_state:(n + 1) * n_state], ari8[:, n * n_state:(n + 1) * n_state])
            for n in range(len(layers))]


def _ssm_layer_kernel(x_hbm, nrm_ref, win_ref, wd_ref, ws_ref, wc_ref, a2r_ref, a2i_ref, d_ref, wglu_ref, bglu_ref,
                      wout_ref, o_hbm, xin_scr, xout_scr, in_sem, out_sem, h_scr, proj_scr, hb_scr, y_scr,
                      *, t_block, n_steps):
    i = pl.program_id(0)
    slot = i % 2
    tm = t_block * BATCH
    n_chunks = t_block // SSM_CHUNK
    cr = n_chunks * BATCH

    def in_copy(step, buf, b):
        return pltpu.make_async_copy(x_hbm.at[b, pl.ds(step * t_block, t_block), :], xin_scr.at[buf, :, b, :],
                                     in_sem.at[buf, b])

    def out_copy(step, buf, b):
        return pltpu.make_async_copy(xout_scr.at[buf, :, b, :], o_hbm.at[b, pl.ds(step * t_block, t_block), :],
                                     out_sem.at[buf, b])

    @pl.when(i == 0)
    def _():
        h_scr[...] = jnp.zeros_like(h_scr)
        for b in range(BATCH):
            in_copy(0, 0, b).start()

    @pl.when(i + 1 < n_steps)
    def _():
        for b in range(BATCH):
            in_copy(i + 1, 1 - slot, b).start()

    for b in range(BATCH):
        in_copy(i, slot, b).wait()

    x = xin_scr[slot].reshape(tm, D_MODEL)
    hn = _rms_scale(x, nrm_ref[...]).astype(BF16)
    proj_scr[...] = jnp.dot(hn, win_ref[...], preferred_element_type=F32).reshape(n_chunks, SSM_CHUNK, BATCH, 2 * BRANCH)

    low = lax.broadcasted_iota(jnp.int32, (1, LANES), 1) < QUAD_CH
    for m in range(BRANCH // LANES):
        ch = slice(m * LANES, (m + 1) * LANES)
        src = [proj_scr[:, e, :, ch].reshape(cr, LANES) for e in range(SSM_CHUNK)]
        swp = [pltpu.roll(v, QUAD_CH, axis=1) for v in src]
        ups = (jnp.concatenate([jnp.where(low, src[0], swp[1]), jnp.where(low, src[2], swp[3])], axis=1),
               jnp.concatenate([jnp.where(low, swp[0], src[1]), jnp.where(low, swp[2], src[3])], axis=1))
        yps = []
        for half, up in enumerate(ups):
            q = 2 * m + half
            st = slice(q * 2 * QUAD_STATE, (q + 1) * 2 * QUAD_STATE)
            hb = hb_scr.at[m % 2, half]
            up = up.astype(BF16)
            yp = jnp.dot(up, wd_ref[q], preferred_element_type=F32)
            hb[BATCH:, :] = jnp.dot(up, ws_ref[q], preferred_element_type=F32)
            hb[:BATCH, :] = h_scr[:, st]
            arr = a2r_ref[:, q * QUAD_STATE:(q + 1) * QUAD_STATE]
            ari = a2i_ref[:, q * QUAD_STATE:(q + 1) * QUAD_STATE]
            hr = hb[:BATCH, :QUAD_STATE]
            hi = hb[:BATCH, QUAD_STATE:]
            for k in range(n_chunks):
                rows = slice((k + 1) * BATCH, (k + 2) * BATCH)
                hr, hi = (arr * hr - ari * hi + hb[rows, :QUAD_STATE], ari * hr + arr * hi + hb[rows, QUAD_STATE:])
                hb[rows, :QUAD_STATE] = hr
                hb[rows, QUAD_STATE:] = hi
            h_scr[:, st] = hb[cr:, :]
            yps.append(yp + jnp.dot(hb[:cr, :].astype(BF16), wc_ref[q], preferred_element_type=F32))
        for n in range(SSM_CHUNK // 2):
            y0 = yps[0][:, n * LANES:(n + 1) * LANES]
            y1 = yps[1][:, n * LANES:(n + 1) * LANES]
            y_scr[:, 2 * n, :, ch] = jnp.where(low, y0, pltpu.roll(y1, QUAD_CH, axis=1)).reshape(n_chunks, BATCH, LANES)
            y_scr[:, 2 * n + 1, :, ch] = jnp.where(low, pltpu.roll(y0, QUAD_CH, axis=1), y1).reshape(n_chunks, BATCH, LANES)

    u = proj_scr[:, :, :, :BRANCH].reshape(tm, BRANCH)
    z = _gelu_tanh(y_scr[...].reshape(tm, BRANCH) + d_ref[...] * u)
    glu = jnp.dot(z.astype(BF16), wglu_ref[...], preferred_element_type=F32) + bglu_ref[...]
    z = z * _sigmoid(glu)
    gate = proj_scr[:, :, :, BRANCH:].reshape(tm, BRANCH)
    act = (z * (gate * _sigmoid(gate))).astype(BF16)
    out = x + jnp.dot(act, wout_ref[...], preferred_element_type=F32)

    @pl.when(i >= 2)
    def _():
        for b in range(BATCH):
            out_copy(i - 2, slot, b).wait()

    xout_scr[slot] = out.reshape(t_block, BATCH, D_MODEL)
    for b in range(BATCH):
        out_copy(i, slot, b).start()

    @pl.when(i == n_steps - 1)
    def _():
        if n_steps >= 2:
            for b in range(BATCH):
                out_copy(i - 1, 1 - slot, b).wait()
        for b in range(BATCH):
            out_copy(i, slot, b).wait()


def _ssm_layer(x_bt, norm, w_in, mixer, d, w_glu, b_glu, w_out):
    bsz, seq, dm = x_bt.shape
    t_block = min(SSM_T_BLOCK, seq)
    n_steps = seq // t_block
    n_chunks = t_block // SSM_CHUNK
    wd, ws, wc, a2r8, a2i8 = mixer
    n_state = SSM_GROUPS * SSM_STATE
    return pl.pallas_call(
        functools.partial(_ssm_layer_kernel, t_block=t_block, n_steps=n_steps),
        out_shape=jax.ShapeDtypeStruct((bsz, seq, dm), F32),
        grid=(n_steps,),
        in_specs=[
            pl.BlockSpec(memory_space=pl.ANY),
            _const_spec((1, dm)),
            _const_spec((dm, 2 * BRANCH)),
            _const_spec(wd.shape),
            _const_spec(ws.shape),
            _const_spec(wc.shape),
            _const_spec((BATCH, n_state)),
            _const_spec((BATCH, n_state)),
            _const_spec((1, BRANCH)),
            _const_spec((BRANCH, BRANCH)),
            _const_spec((1, BRANCH)),
            _const_spec((BRANCH, dm)),
        ],
        out_specs=pl.BlockSpec(memory_space=pl.ANY),
        scratch_shapes=[
            pltpu.VMEM((2, t_block, BATCH, dm), F32),
            pltpu.VMEM((2, t_block, BATCH, dm), F32),
            pltpu.SemaphoreType.DMA((2, BATCH)),
            pltpu.SemaphoreType.DMA((2, BATCH)),
            pltpu.VMEM((BATCH, 2 * n_state), F32),
            pltpu.VMEM((n_chunks, SSM_CHUNK, BATCH, 2 * BRANCH), F32),
            pltpu.VMEM((2, 2, (n_chunks + 1) * BATCH, 2 * QUAD_STATE), F32),
            pltpu.VMEM((n_chunks, SSM_CHUNK, BATCH, BRANCH), F32),
        ],
        compiler_params=pltpu.CompilerParams(dimension_semantics=("arbitrary",), vmem_limit_bytes=VMEM_LIMIT),
        name="ssm_layer",
    )(x_bt, norm.astype(F32).reshape(1, dm), w_in.astype(BF16), wd, ws, wc, a2r8, a2i8,
      d.astype(F32).reshape(1, BRANCH), w_glu.astype(BF16), b_glu.astype(F32).reshape(1, BRANCH), w_out.astype(BF16))


def _rope_kernel(freq_ref, cos_ref, sin_ref):
    rows = cos_ref.shape[0]
    pos = (lax.broadcasted_iota(jnp.int32, cos_ref.shape, 0) + pl.program_id(0) * rows).astype(F32)
    ang = pos * freq_ref[...]
    lane = lax.broadcasted_iota(jnp.int32, cos_ref.shape, 1)
    sin = jnp.sin(ang)
    cos_ref[...] = jnp.cos(ang)
    sin_ref[...] = jnp.where(lane < LANES // 2, -sin, sin)


def _rope_tables(seq):
    inv_freq = (np.float32(ROPE_THETA) ** (-np.arange(0, HEAD_DIM, 2, dtype=np.float32) / np.float32(HEAD_DIM))).astype(np.float32)
    freq = jnp.asarray(np.tile(inv_freq, LANES // HALF).reshape(1, LANES))
    blk = min(1024, seq)
    spec = pl.BlockSpec((blk, LANES), lambda i: (i, 0))
    shp = jax.ShapeDtypeStruct((seq, LANES), F32)
    return pl.pallas_call(
        _rope_kernel, out_shape=(shp, shp), grid=(seq // blk,),
        in_specs=[pl.BlockSpec((1, LANES), lambda i: (0, 0))], out_specs=(spec, spec), name="rope_tables",
    )(freq)


def _attn_weight_layout(w_in):
    wq = w_in[:, :BRANCH] * (HEAD_DIM ** -0.5)
    wk = w_in[:, BRANCH:BRANCH + N_KV_HEADS * HEAD_DIM]
    wv = w_in[:, BRANCH + N_KV_HEADS * HEAD_DIM:BRANCH + 2 * N_KV_HEADS * HEAD_DIM]
    wg = w_in[:, BRANCH + 2 * N_KV_HEADS * HEAD_DIM:]
    dm = w_in.shape[0]
    wq = wq.reshape(dm, N_Q_HEADS // 2, 2, 2, HALF).transpose(0, 1, 3, 2, 4).reshape(dm, BRANCH)
    wk = jnp.broadcast_to(wk.reshape(dm, N_KV_HEADS, 2, 1, HALF), (dm, N_KV_HEADS, 2, 2, HALF))
    wk = wk.reshape(dm, N_KV_HEADS * LANES)
    wv = jnp.broadcast_to(wv.reshape(dm, N_KV_HEADS, 1, HEAD_DIM), (dm, N_KV_HEADS, 2, HEAD_DIM))
    wv = wv.reshape(dm, N_KV_HEADS * LANES)
    return jnp.concatenate([wq, wk, wv, wg], axis=1).astype(BF16)


_Q0 = 0
_K0 = BRANCH
_V0 = BRANCH + N_KV_HEADS * LANES
_G0 = BRANCH + 2 * N_KV_HEADS * LANES
_PROJ_COLS = _G0 + BRANCH
PAIRS_PER_KV = N_Q_HEADS // 2 // N_KV_HEADS
PROJ_TILE = 256


def _attn_layer_kernel(sinks_ref, xc_ref, xn_ref, nrm_ref, win_ref, cosc_ref, sinc_ref, cosn_ref, sinn_ref, wout_ref,
                       fin_ref, o_ref,
                       proj_scr, hn_scr, q_scr, kz_scr, vz_scr, o_scr, bias_scr, s_scr, mx_scr, es_scr,
                       *, rows, n_t, n_steps, final_norm):
    step = pl.program_id(0) * n_t + pl.program_id(1)
    ti = pl.program_id(1)
    nxt = jnp.minimum(step + 1, n_steps - 1)
    t_nxt = lax.rem(nxt, n_t)
    n_blk = rows // ATTN_BLOCK
    n_units = N_KV_HEADS * n_blk
    stack = PAIRS_PER_KV * ATTN_BLOCK
    window = 2 * ATTN_BLOCK

    n_tiles = _PROJ_COLS // PROJ_TILE

    def project_tiles(x_ref, par):
        def norm_and_tile(k):
            if k == 0:
                hn_scr[...] = _rms_scale(x_ref[...], nrm_ref[...]).astype(BF16)
            cols = slice(k * PROJ_TILE, (k + 1) * PROJ_TILE)
            proj_scr[par, :, cols] = jnp.dot(hn_scr[...], win_ref[:, cols], preferred_element_type=F32)
        return [functools.partial(norm_and_tile, k) for k in range(n_tiles)]

    def finish_projection(cos_ref, sin_ref, par, prev, t_blk):
        cos = cos_ref[...]
        sin = sin_ref[...]

        def rope(t):
            return t * cos + pltpu.roll(t, LANES // 2, axis=1) * sin

        for r in range(N_Q_HEADS // 2):
            q_scr[par, r] = rope(proj_scr[par, :, _Q0 + r * LANES:_Q0 + (r + 1) * LANES]).astype(BF16)
        keep = t_blk > 0
        if prev is None:
            kz_scr[par, :, :, :ATTN_BLOCK, :] = jnp.zeros((N_KV_HEADS, 2, ATTN_BLOCK, LANES), BF16)
            vz_scr[par, :, :, :ATTN_BLOCK, :] = jnp.zeros((N_KV_HEADS, 2, ATTN_BLOCK, LANES), BF16)
        else:
            kz_scr[par, :, :, :ATTN_BLOCK, :] = jnp.where(keep, kz_scr[prev, :, :, rows:, :], 0.0).astype(BF16)
            vz_scr[par, :, :, :ATTN_BLOCK, :] = jnp.where(keep, vz_scr[prev, :, :, rows:, :], 0.0).astype(BF16)
        lane = lax.broadcasted_iota(jnp.int32, (1, LANES), 1)
        for kv in range(N_KV_HEADS):
            k_rep = rope(proj_scr[par, :, _K0 + kv * LANES:_K0 + (kv + 1) * LANES])
            v_rep = proj_scr[par, :, _V0 + kv * LANES:_V0 + (kv + 1) * LANES]
            for slot in range(2):
                k_mask = (lane // HALF) % 2 == slot
                v_mask = lane // HEAD_DIM == slot
                kz_scr[par, kv, slot, ATTN_BLOCK:, :] = jnp.where(k_mask, k_rep, 0.0).astype(BF16)
                vz_scr[par, kv, slot, ATTN_BLOCK:, :] = jnp.where(v_mask, v_rep, 1.0).astype(BF16)

    def attend(par, fillers):
        def scores(u):
            kv, i = divmod(u, n_blk)
            row0 = i * ATTN_BLOCK
            first = jnp.logical_and(ti == 0, i == 0).astype(jnp.int32)
            q_stack = jnp.concatenate(
                [q_scr[par, kv * PAIRS_PER_KV + n, pl.ds(row0, ATTN_BLOCK), :] for n in range(PAIRS_PER_KV)], axis=0)
            for slot in range(2):
                sink = jnp.concatenate(
                    [jnp.full((ATTN_BLOCK, LANES), sinks_ref[2 * (kv * PAIRS_PER_KV + n) + slot], F32)
                     for n in range(PAIRS_PER_KV)], axis=0)
                s = lax.dot_general(q_stack, kz_scr[par, kv, slot, pl.ds(row0, window), :], (((1,), (1,)), ((), ())),
                                    preferred_element_type=F32) + bias_scr[first]
                mx = jnp.maximum(jnp.broadcast_to(jnp.max(s, axis=-1, keepdims=True), (stack, LANES)), sink)
                s_scr[u % 2, slot] = s
                mx_scr[u % 2, slot] = mx
                es_scr[u % 2, slot] = sink - mx

        def values(u):
            kv, i = divmod(u, n_blk)
            row0 = i * ATTN_BLOCK
            outs = []
            for slot in range(2):
                mx = mx_scr[u % 2, slot]
                p = jnp.exp(s_scr[u % 2, slot] - jnp.concatenate([mx, mx], axis=1)).astype(BF16)
                outs.append(jnp.dot(p, vz_scr[par, kv, slot, pl.ds(row0, window), :], preferred_element_type=F32))
            low = lax.broadcasted_iota(jnp.int32, (1, LANES), 1) < HEAD_DIM
            num = jnp.where(low, outs[0], outs[1])
            sums = pltpu.roll(jnp.where(low, outs[1], outs[0]), HEAD_DIM, axis=1)
            den = sums + jnp.exp(jnp.where(low, es_scr[u % 2, 0], es_scr[u % 2, 1]))
            o = num / den
            for n in range(PAIRS_PER_KV):
                o_scr[kv * PAIRS_PER_KV + n, pl.ds(row0, ATTN_BLOCK), :] = o[n * ATTN_BLOCK:(n + 1) * ATTN_BLOCK]

        scores(0)
        for u in range(1, n_units + 1):
            for f in fillers[(u - 1) * len(fillers) // n_units:u * len(fillers) // n_units]:
                f()
            values(u - 1)
            if u < n_units:
                scores(u)

        gate = proj_scr[par, :, _G0:]
        attn = jnp.concatenate([o_scr[r] for r in range(N_Q_HEADS // 2)], axis=1)
        act = (attn * (gate * _sigmoid(gate))).astype(BF16)
        y = xc_ref[...] + jnp.dot(act, wout_ref[...], preferred_element_type=F32)
        if final_norm:
            y = _rms_scale(y, fin_ref[...])
        o_ref[...] = y

    @pl.when(step == 0)
    def _():
        qi = lax.broadcasted_iota(jnp.int32, (stack, window), 0) % ATTN_BLOCK
        kj = lax.broadcasted_iota(jnp.int32, (stack, window), 1)
        dist = qi + ATTN_BLOCK - kj
        band = (dist >= 0) & (dist < ATTN_BLOCK)
        bias_scr[0] = jnp.where(band, 0.0, NEG_INF)
        bias_scr[1] = jnp.where(band & (kj >= ATTN_BLOCK), 0.0, NEG_INF)

    @pl.when(step == 0)
    def _():
        for f in project_tiles(xc_ref, 0):
            f()
        finish_projection(cosc_ref, sinc_ref, 0, None, 0)

    for par in range(2):
        @pl.when(lax.rem(step, 2) == par)
        def _(par=par):
            attend(par, project_tiles(xn_ref, 1 - par))
            finish_projection(cosn_ref, sinn_ref, 1 - par, par, t_nxt)


def _attn_layer(x_bt, norm, win, sinks, w_out, cos, sin, fin, *, final_norm):
    bsz, seq, dm = x_bt.shape
    rows = min(ATTN_ROWS, seq)
    n_t = seq // rows
    n_steps = bsz * n_t
    stack = PAIRS_PER_KV * ATTN_BLOCK

    def nxt(b, t):
        f = jnp.minimum(b * n_t + t + 1, n_steps - 1)
        return f // n_t, f % n_t

    cur_spec = pl.BlockSpec((None, rows, dm), lambda b, t: (b, t, 0))
    nxt_spec = pl.BlockSpec((None, rows, dm), lambda b, t: (*nxt(b, t), 0))
    tab_cur = pl.BlockSpec((rows, LANES), lambda b, t: (t, 0))
    tab_nxt = pl.BlockSpec((rows, LANES), lambda b, t: (nxt(b, t)[1], 0))
    return pl.pallas_call(
        functools.partial(_attn_layer_kernel, rows=rows, n_t=n_t, n_steps=n_steps, final_norm=final_norm),
        out_shape=jax.ShapeDtypeStruct((bsz, seq, dm), F32),
        grid=(bsz, n_t),
        in_specs=[
            pl.BlockSpec(memory_space=pltpu.SMEM),
            cur_spec,
            nxt_spec,
            _const_spec((1, dm)),
            _const_spec((dm, _PROJ_COLS)),
            tab_cur,
            tab_cur,
            tab_nxt,
            tab_nxt,
            _const_spec((BRANCH, dm)),
            _const_spec((1, dm)),
        ],
        out_specs=cur_spec,
        scratch_shapes=[
            pltpu.VMEM((2, rows, _PROJ_COLS), F32),
            pltpu.VMEM((rows, D_MODEL), BF16),
            pltpu.VMEM((2, N_Q_HEADS // 2, rows, LANES), BF16),
            pltpu.VMEM((2, N_KV_HEADS, 2, ATTN_BLOCK + rows, LANES), BF16),
            pltpu.VMEM((2, N_KV_HEADS, 2, ATTN_BLOCK + rows, LANES), BF16),
            pltpu.VMEM((N_Q_HEADS // 2, rows, LANES), F32),
            pltpu.VMEM((2, stack, 2 * ATTN_BLOCK), F32),
            pltpu.VMEM((2, 2, stack, 2 * ATTN_BLOCK), F32),
            pltpu.VMEM((2, 2, stack, LANES), F32),
            pltpu.VMEM((2, 2, stack, LANES), F32),
        ],
        compiler_params=pltpu.CompilerParams(dimension_semantics=("arbitrary", "arbitrary"), vmem_limit_bytes=VMEM_LIMIT),
        name="attn_layer",
    )(sinks.astype(F32), x_bt, x_bt, norm.astype(F32).reshape(1, dm), win, cos, sin, cos, sin, w_out.astype(BF16),
      fin.astype(F32).reshape(1, dm))


def kernel(x, l0_norm, l0_w_in, l0_a_re, l0_a_im, l0_log_step, l0_b_re, l0_b_im, l0_c_re, l0_c_im, l0_d, l0_w_glu, l0_b_glu, l0_w_out, l1_norm, l1_w_in, l1_sinks, l1_w_out, l2_norm, l2_w_in, l2_a_re, l2_a_im, l2_log_step, l2_b_re, l2_b_im, l2_c_re, l2_c_im, l2_d, l2_w_glu, l2_b_glu, l2_w_out, l3_norm, l3_w_in, l3_sinks, l3_w_out, final_norm):
    bsz, seq, _ = x.shape
    assert bsz == BATCH
    cos, sin = _rope_tables(seq)
    mix0, mix2 = _ssm_weights([(l0_a_re, l0_a_im, l0_log_step, l0_b_re, l0_b_im, l0_c_re, l0_c_im),
                               (l2_a_re, l2_a_im, l2_log_step, l2_b_re, l2_b_im, l2_c_re, l2_c_im)])
    win1, win3 = jax.vmap(_attn_weight_layout)(jnp.stack([l1_w_in, l3_w_in]).astype(F32))
    h = x.astype(F32)
    h = _ssm_layer(h, l0_norm, l0_w_in, mix0, l0_d, l0_w_glu, l0_b_glu, l0_w_out)
    h = _attn_layer(h, l1_norm, win1, l1_sinks, l1_w_out, cos, sin, final_norm, final_norm=False)
    h = _ssm_layer(h, l2_norm, l2_w_in, mix2, l2_d, l2_w_glu, l2_b_glu, l2_w_out)
    h = _attn_layer(h, l3_norm, win3, l3_sinks, l3_w_out, cos, sin, final_norm, final_norm=True)
    return h.astype(x.dtype)
```

```python
import functools
import math

import numpy as np
import jax
import jax.numpy as jnp
from jax import lax
from jax.experimental import pallas as pl
from jax.experimental.pallas import tpu as pltpu

D_MODEL = 1024
BATCH = 8
BRANCH = D_MODEL
SSM_GROUP = 16
SSM_GROUPS = BRANCH // SSM_GROUP
SSM_STATE = 64
HEAD_DIM = 64
N_Q_HEADS = BRANCH // HEAD_DIM
N_KV_HEADS = 2
ATTN_BLOCK = 128
ROPE_THETA = 10000.0
NORM_EPS = 1e-5
NEG_INF = -1e30

LANES = 128
SSM_CHUNK = 4
QUAD_CH = LANES // 2
QUAD_GROUPS = QUAD_CH // SSM_GROUP
N_QUADS = BRANCH // QUAD_CH
QUAD_STATE = QUAD_GROUPS * SSM_STATE
HALF = HEAD_DIM // 2

SSM_T_BLOCK = 128
ATTN_ROWS = 512
VMEM_LIMIT = 58 * 1024 * 1024

F32 = jnp.float32
BF16 = jnp.bfloat16


def _sigmoid(v):
    return 1.0 / (1.0 + jnp.exp(-v))


def _gelu_tanh(v):
    c = math.sqrt(2.0 / math.pi)
    return 0.5 * v * (1.0 + jnp.tanh(c * (v + 0.044715 * (v * v * v))))


def _rms_scale(x, g):
    ms = jnp.mean(x * x, axis=-1, keepdims=True)
    return x * lax.rsqrt(ms + NORM_EPS) * g


def _const_spec(shape):
    nd = len(shape)
    return pl.BlockSpec(shape, lambda *_: (0,) * nd, pipeline_mode=pl.Buffered(1))


def _ssm_prep_kernel(are_ref, aim_ref, ls_ref, bre_ref, bim_ref, abr_ref, abi_ref, bbr_ref, bbi_ref):
    lr = are_ref[...]
    li = aim_ref[...]
    step = jnp.exp(ls_ref[...])
    mag = jnp.exp(lr * step)
    ar = mag * jnp.cos(li * step)
    ai = mag * jnp.sin(li * step)
    xr = ar - 1.0
    den = lr * lr + li * li
    cr = (xr * lr + ai * li) / den
    ci = (ai * lr - xr * li) / den
    br = bre_ref[...]
    bi = bim_ref[...]
    abr_ref[...] = ar
    abi_ref[...] = ai
    bbr_ref[...] = cr * br - ci * bi
    bbi_ref[...] = cr * bi + ci * br


def _ssm_prep(a_re, a_im, log_step, b_re, b_im):
    g, p, c = b_re.shape
    rep = lambda a: jnp.repeat(a.astype(F32), c, axis=1)
    ls = jnp.broadcast_to(log_step.astype(F32)[:, None], (g, p * c))
    shp = jax.ShapeDtypeStruct((g, p * c), F32)
    abr, abi, bbr, bbi = pl.pallas_call(
        _ssm_prep_kernel, out_shape=(shp, shp, shp, shp), name="ssm_prep",
    )(rep(a_re), rep(a_im), ls, b_re.astype(F32).reshape(g, p * c), b_im.astype(F32).reshape(g, p * c))
    a_bar_re = abr.reshape(g, p, c)[:, :, 0]
    a_bar_im = abi.reshape(g, p, c)[:, :, 0]
    return a_bar_re, a_bar_im, bbr.reshape(g, p, c), bbi.reshape(g, p, c)


def _cmul(ar, ai, br, bi):
    return ar * br - ai * bi, ar * bi + ai * br


def _ssm_matrices(a_re, a_im, bb_re, bb_im, c_re, c_im):
    hi_p = lax.Precision.HIGHEST
    r = SSM_CHUNK
    cr = c_re.astype(F32)
    ci = c_im.astype(F32)
    pw = [(jnp.ones_like(a_re), jnp.zeros_like(a_im))]
    for _ in range(r):
        pw.append(_cmul(pw[-1][0], pw[-1][1], a_re, a_im))
    ca = [_cmul(cr, ci, p_r[:, None, :], p_i[:, None, :]) for p_r, p_i in pw]
    ab = [_cmul(p_r[:, :, None], p_i[:, :, None], bb_re, bb_im) for p_r, p_i in pw[:r]]
    kd = [jnp.einsum("gcp,gpd->gcd", ca[k][0], bb_re, precision=hi_p)
          - jnp.einsum("gcp,gpd->gcd", ca[k][1], bb_im, precision=hi_p) for k in range(r)]

    diag = jnp.eye(QUAD_GROUPS, dtype=bool)[None, None, :, None, None, :, None]
    n_quads = a_re.shape[0] // QUAD_GROUPS
    quad_shape = (n_quads, QUAD_GROUPS)
    zero = jnp.zeros_like(kd[0])
    t = jnp.stack([jnp.stack([kd[d - e] if d >= e else zero for d in range(r)]) for e in range(r)])
    t = t.reshape(r, r, *quad_shape, SSM_GROUP, SSM_GROUP).transpose(2, 0, 5, 1, 3, 4)
    w_direct = jnp.where(diag, t[:, :, None], 0.0).reshape(n_quads, r * QUAD_CH, r * QUAD_CH)
    s = jnp.stack([jnp.stack(ab[r - 1 - e]) for e in range(r)])
    s = s.reshape(r, 2, *quad_shape, SSM_STATE, SSM_GROUP).transpose(2, 0, 5, 1, 3, 4)
    w_state = jnp.where(diag, s[:, :, None], 0.0).reshape(n_quads, r * QUAD_CH, 2 * QUAD_STATE)
    m = jnp.stack([jnp.stack([ca[d + 1][0], -ca[d + 1][1]]) for d in range(r)])
    m = m.reshape(r, 2, *quad_shape, SSM_GROUP, SSM_STATE).transpose(2, 1, 3, 5, 0, 4)
    w_carry = jnp.where(diag, m[:, :, :, :, :, None, :], 0.0).reshape(n_quads, 2 * QUAD_STATE, r * QUAD_CH)
    arr8 = jnp.broadcast_to(pw[r][0].reshape(1, -1), (BATCH, a_re.size))
    ari8 = jnp.broadcast_to(pw[r][1].reshape(1, -1), (BATCH, a_re.size))
    return w_direct.astype(BF16), w_state.astype(BF16), w_carry.astype(BF16), arr8, ari8


def _ssm_weights(layers):
    cat = [jnp.concatenate([lay[k].astype(F32) for lay in layers], axis=0) for k in range(7)]
    a_bar_re, a_bar_im, bb_re, bb_im = _ssm_prep(*cat[:5])
    wd, ws, wc, arr8, ari8 = _ssm_matrices(a_bar_re, a_bar_im, bb_re, bb_im, cat[5], cat[6])
    n_state = SSM_GROUPS * SSM_STATE
    return [(wd[n * N_QUADS:(n + 1) * N_QUADS], ws[n * N_QUADS:(n + 1) * N_QUADS], wc[n * N_QUADS:(n + 1) * N_QUADS],
             arr8[:, n * n_state:(n + 1) * n_state], ari8[:, n * n_state:(n + 1) * n_state])
            for n in range(len(layers))]


def _ssm_layer_kernel(x_hbm, nrm_ref, win_ref, wd_ref, ws_ref, wc_ref, a2r_ref, a2i_ref, d_ref, wglu_ref, bglu_ref,
                      wout_ref, o_hbm, xin_scr, xout_scr, in_sem, out_sem, h_scr, pu_scr, gate_scr, hn_scr, hb_scr,
                      *, t_block, n_steps):
    i = pl.program_id(0)
    slot = i % 2
    nxt = jnp.minimum(i + 1, n_steps - 1)
    tm = t_block * BATCH
    n_chunks = t_block // SSM_CHUNK
    cr = n_chunks * BATCH
    n_tiles = BRANCH // PROJ_TILE

    def in_copy(step, buf, b):
        return pltpu.make_async_copy(x_hbm.at[b, pl.ds(step * t_block, t_block), :], xin_scr.at[buf, :, b, :],
                                     in_sem.at[buf, b])

    def out_copy(step, b):
        return pltpu.make_async_copy(xout_scr.at[:, b, :], o_hbm.at[b, pl.ds(step * t_block, t_block), :],
                                     out_sem.at[b])

    def u_tiles(par):
        def norm_and_tile(k):
            if k == 0:
                hn_scr[par] = _rms_scale(xin_scr[par].reshape(tm, D_MODEL), nrm_ref[...]).astype(BF16)
            cols = slice(k * PROJ_TILE, (k + 1) * PROJ_TILE)
            pu_scr[par, :, :, :, cols] = jnp.dot(hn_scr[par], win_ref[:, cols], preferred_element_type=F32).reshape(
                n_chunks, SSM_CHUNK, BATCH, PROJ_TILE)
        return [functools.partial(norm_and_tile, k) for k in range(n_tiles)]

    def gate_tiles(par):
        def tile(k):
            cols = slice(k * PROJ_TILE, (k + 1) * PROJ_TILE)
            gate_scr[:, cols] = jnp.dot(hn_scr[par], win_ref[:, BRANCH + k * PROJ_TILE:BRANCH + (k + 1) * PROJ_TILE],
                                        preferred_element_type=F32)
        return [functools.partial(tile, k) for k in range(n_tiles)]

    def mix(par, fillers):
        low = lax.broadcasted_iota(jnp.int32, (1, LANES), 1) < QUAD_CH
        n_cols = BRANCH // LANES
        for m in range(n_cols):
            ch = slice(m * LANES, (m + 1) * LANES)
            src = [pu_scr[par, :, e, :, ch].reshape(cr, LANES) for e in range(SSM_CHUNK)]
            swp = [pltpu.roll(v, QUAD_CH, axis=1) for v in src]
            ups = (jnp.concatenate([jnp.where(low, src[0], swp[1]), jnp.where(low, src[2], swp[3])], axis=1),
                   jnp.concatenate([jnp.where(low, swp[0], src[1]), jnp.where(low, swp[2], src[3])], axis=1))
            yps = []
            for half, up in enumerate(ups):
                q = 2 * m + half
                st = slice(q * 2 * QUAD_STATE, (q + 1) * 2 * QUAD_STATE)
                hb = hb_scr.at[m % 2, half]
                up = up.astype(BF16)
                yp = jnp.dot(up, wd_ref[q], preferred_element_type=F32)
                hb[BATCH:, :] = jnp.dot(up, ws_ref[q], preferred_element_type=F32)
                hb[:BATCH, :] = h_scr[:, st]
                arr = a2r_ref[:, q * QUAD_STATE:(q + 1) * QUAD_STATE]
                ari = a2i_ref[:, q * QUAD_STATE:(q + 1) * QUAD_STATE]
                hr = hb[:BATCH, :QUAD_STATE]
                hi = hb[:BATCH, QUAD_STATE:]
                for k in range(n_chunks):
                    rows = slice((k + 1) * BATCH, (k + 2) * BATCH)
                    hr, hi = (arr * hr - ari * hi + hb[rows, :QUAD_STATE], ari * hr + arr * hi + hb[rows, QUAD_STATE:])
                    hb[rows, :QUAD_STATE] = hr
                    hb[rows, QUAD_STATE:] = hi
                h_scr[:, st] = hb[cr:, :]
                yps.append(yp + jnp.dot(hb[:cr, :].astype(BF16), wc_ref[q], preferred_element_type=F32))
            for n in range(SSM_CHUNK // 2):
                y0 = yps[0][:, n * LANES:(n + 1) * LANES]
                y1 = yps[1][:, n * LANES:(n + 1) * LANES]
                even = jnp.where(low, y0, pltpu.roll(y1, QUAD_CH, axis=1)) + d_ref[:, ch] * src[2 * n]
                odd = jnp.where(low, pltpu.roll(y0, QUAD_CH, axis=1), y1) + d_ref[:, ch] * src[2 * n + 1]
                pu_scr[par, :, 2 * n, :, ch] = even.reshape(n_chunks, BATCH, LANES)
                pu_scr[par, :, 2 * n + 1, :, ch] = odd.reshape(n_chunks, BATCH, LANES)
            for f in fillers[m * len(fillers) // n_cols:(m + 1) * len(fillers) // n_cols]:
                f()

        x = xin_scr[par].reshape(tm, D_MODEL)
        z = _gelu_tanh(pu_scr[par].reshape(tm, BRANCH))
        glu = jnp.dot(z.astype(BF16), wglu_ref[...], preferred_element_type=F32) + bglu_ref[...]
        z = z * _sigmoid(glu)
        gate = gate_scr[...]
        act = (z * (gate * _sigmoid(gate))).astype(BF16)
        return x + jnp.dot(act, wout_ref[...], preferred_element_type=F32)

    @pl.when(i == 0)
    def _():
        h_scr[...] = jnp.zeros_like(h_scr)
        for b in range(BATCH):
            in_copy(0, 0, b).start()
        for b in range(BATCH):
            in_copy(0, 0, b).wait()
        for f in u_tiles(0):
            f()

    for b in range(BATCH):
        in_copy(nxt, 1 - slot, b).start()

    @pl.when(i >= 1)
    def _():
        for b in range(BATCH):
            out_copy(i - 1, b).wait()

    for par in range(2):
        @pl.when(slot == par)
        def _(par=par):
            def wait_next():
                for b in range(BATCH):
                    in_copy(nxt, 1 - par, b).wait()
            out = mix(par, gate_tiles(par) + [wait_next] + u_tiles(1 - par))
            xout_scr[...] = out.reshape(t_block, BATCH, D_MODEL)

    for b in range(BATCH):
        out_copy(i, b).start()

    @pl.when(i == n_steps - 1)
    def _():
        for b in range(BATCH):
            out_copy(i, b).wait()


def _ssm_layer(x_bt, norm, w_in, mixer, d, w_glu, b_glu, w_out):
    bsz, seq, dm = x_bt.shape
    t_block = min(SSM_T_BLOCK, seq)
    n_steps = seq // t_block
    n_chunks = t_block // SSM_CHUNK
    wd, ws, wc, a2r8, a2i8 = mixer
    n_state = SSM_GROUPS * SSM_STATE
    return pl.pallas_call(
        functools.partial(_ssm_layer_kernel, t_block=t_block, n_steps=n_steps),
        out_shape=jax.ShapeDtypeStruct((bsz, seq, dm), F32),
        grid=(n_steps,),
        in_specs=[
            pl.BlockSpec(memory_space=pl.ANY),
            _const_spec((1, dm)),
            _const_spec((dm, 2 * BRANCH)),
            _const_spec(wd.shape),
            _const_spec(ws.shape),
            _const_spec(wc.shape),
            _const_spec((BATCH, n_state)),
            _const_spec((BATCH, n_state)),
            _const_spec((1, BRANCH)),
            _const_spec((BRANCH, BRANCH)),
            _const_spec((1, BRANCH)),
            _const_spec((BRANCH, dm)),
        ],
        out_specs=pl.BlockSpec(memory_space=pl.ANY),
        scratch_shapes=[
            pltpu.VMEM((2, t_block, BATCH, dm), F32),
            pltpu.VMEM((t_block, BATCH, dm), F32),
            pltpu.SemaphoreType.DMA((2, BATCH)),
            pltpu.SemaphoreType.DMA((BATCH,)),
            pltpu.VMEM((BATCH, 2 * n_state), F32),
            pltpu.VMEM((2, n_chunks, SSM_CHUNK, BATCH, BRANCH), F32),
            pltpu.VMEM((t_block * BATCH, BRANCH), F32),
            pltpu.VMEM((2, t_block * BATCH, dm), BF16),
            pltpu.VMEM((2, 2, (n_chunks + 1) * BATCH, 2 * QUAD_STATE), F32),
        ],
        compiler_params=pltpu.CompilerParams(dimension_semantics=("arbitrary",), vmem_limit_bytes=VMEM_LIMIT),
        name="ssm_layer",
    )(x_bt, norm.astype(F32).reshape(1, dm), w_in.astype(BF16), wd, ws, wc, a2r8, a2i8,
      d.astype(F32).reshape(1, BRANCH), w_glu.astype(BF16), b_glu.astype(F32).reshape(1, BRANCH), w_out.astype(BF16))


def _rope_kernel(freq_ref, cos_ref, sin_ref):
    rows = cos_ref.shape[0]
    pos = (lax.broadcasted_iota(jnp.int32, cos_ref.shape, 0) + pl.program_id(0) * rows).astype(F32)
    ang = pos * freq_ref[...]
    lane = lax.broadcasted_iota(jnp.int32, cos_ref.shape, 1)
    sin = jnp.sin(ang)
    cos_ref[...] = jnp.cos(ang)
    sin_ref[...] = jnp.where(lane < LANES // 2, -sin, sin)


def _rope_tables(seq):
    inv_freq = (np.float32(ROPE_THETA) ** (-np.arange(0, HEAD_DIM, 2, dtype=np.float32) / np.float32(HEAD_DIM))).astype(np.float32)
    freq = jnp.asarray(np.tile(inv_freq, LANES // HALF).reshape(1, LANES))
    blk = min(1024, seq)
    spec = pl.BlockSpec((blk, LANES), lambda i: (i, 0))
    shp = jax.ShapeDtypeStruct((seq, LANES), F32)
    return pl.pallas_call(
        _rope_kernel, out_shape=(shp, shp), grid=(seq // blk,),
        in_specs=[pl.BlockSpec((1, LANES), lambda i: (0, 0))], out_specs=(spec, spec), name="rope_tables",
    )(freq)


def _attn_weight_layout(w_in):
    wq = w_in[:, :BRANCH] * (HEAD_DIM ** -0.5)
    wk = w_in[:, BRANCH:BRANCH + N_KV_HEADS * HEAD_DIM]
    wv = w_in[:, BRANCH + N_KV_HEADS * HEAD_DIM:BRANCH + 2 * N_KV_HEADS * HEAD_DIM]
    wg = w_in[:, BRANCH + 2 * N_KV_HEADS * HEAD_DIM:]
    dm = w_in.shape[0]
    wq = wq.reshape(dm, N_Q_HEADS // 2, 2, 2, HALF).transpose(0, 1, 3, 2, 4).reshape(dm, BRANCH)
    wk = jnp.broadcast_to(wk.reshape(dm, N_KV_HEADS, 2, 1, HALF), (dm, N_KV_HEADS, 2, 2, HALF))
    wk = wk.reshape(dm, N_KV_HEADS * LANES)
    wv = jnp.broadcast_to(wv.reshape(dm, N_KV_HEADS, 1, HEAD_DIM), (dm, N_KV_HEADS, 2, HEAD_DIM))
    wv = wv.reshape(dm, N_KV_HEADS * LANES)
    return jnp.concatenate([wq, wk, wv, wg], axis=1).astype(BF16)


_Q0 = 0
_K0 = BRANCH
_V0 = BRANCH + N_KV_HEADS * LANES
_G0 = BRANCH + 2 * N_KV_HEADS * LANES
_PROJ_COLS = _G0 + BRANCH
PAIRS_PER_KV = N_Q_HEADS // 2 // N_KV_HEADS
PROJ_TILE = 256


def _attn_layer_kernel(sinks_ref, xc_ref, xn_ref, nrm_ref, win_ref, cosc_ref, sinc_ref, cosn_ref, sinn_ref, wout_ref,
                       fin_ref, o_ref,
                       proj_scr, hn_scr, q_scr, kz_scr, vz_scr, o_scr, bias_scr, s_scr, mx_scr, es_scr,
                       *, rows, n_t, n_steps, final_norm):
    step = pl.program_id(0) * n_t + pl.program_id(1)
    ti = pl.program_id(1)
    nxt = jnp.minimum(step + 1, n_steps - 1)
    t_nxt = lax.rem(nxt, n_t)
    n_blk = rows // ATTN_BLOCK
    n_units = N_KV_HEADS * n_blk
    stack = PAIRS_PER_KV * ATTN_BLOCK
    window = 2 * ATTN_BLOCK

    n_tiles = _PROJ_COLS // PROJ_TILE

    def project_tiles(x_ref, par):
        def norm_and_tile(k):
            if k == 0:
                hn_scr[...] = _rms_scale(x_ref[...], nrm_ref[...]).astype(BF16)
            cols = slice(k * PROJ_TILE, (k + 1) * PROJ_TILE)
            proj_scr[par, :, cols] = jnp.dot(hn_scr[...], win_ref[:, cols], preferred_element_type=F32)
        return [functools.partial(norm_and_tile, k) for k in range(n_tiles)]

    def finish_projection(cos_ref, sin_ref, par, prev, t_blk):
        cos = cos_ref[...]
        sin = sin_ref[...]

        def rope(t):
            return t * cos + pltpu.roll(t, LANES // 2, axis=1) * sin

        for r in range(N_Q_HEADS // 2):
            q_scr[par, r] = rope(proj_scr[par, :, _Q0 + r * LANES:_Q0 + (r + 1) * LANES]).astype(BF16)
        keep = t_blk > 0
        if prev is None:
            kz_scr[par, :, :, :ATTN_BLOCK, :] = jnp.zeros((N_KV_HEADS, 2, ATTN_BLOCK, LANES), BF16)
            vz_scr[par, :, :, :ATTN_BLOCK, :] = jnp.zeros((N_KV_HEADS, 2, ATTN_BLOCK, LANES), BF16)
        else:
            kz_scr[par, :, :, :ATTN_BLOCK, :] = jnp.where(keep, kz_scr[prev, :, :, rows:, :], 0.0).astype(BF16)
            vz_scr[par, :, :, :ATTN_BLOCK, :] = jnp.where(keep, vz_scr[prev, :, :, rows:, :], 0.0).astype(BF16)
        lane = lax.broadcasted_iota(jnp.int32, (1, LANES), 1)
        for kv in range(N_KV_HEADS):
            k_rep = rope(proj_scr[par, :, _K0 + kv * LANES:_K0 + (kv + 1) * LANES])
            v_rep = proj_scr[par, :, _V0 + kv * LANES:_V0 + (kv + 1) * LANES]
            for slot in range(2):
                k_mask = (lane // HALF) % 2 == slot
                v_mask = lane // HEAD_DIM == slot
                kz_scr[par, kv, slot, ATTN_BLOCK:, :] = jnp.where(k_mask, k_rep, 0.0).astype(BF16)
                vz_scr[par, kv, slot, ATTN_BLOCK:, :] = jnp.where(v_mask, v_rep, 1.0).astype(BF16)

    def attend(par, fillers):
        def scores(u):
            kv, i = divmod(u, n_blk)
            row0 = i * ATTN_BLOCK
            first = jnp.logical_and(ti == 0, i == 0).astype(jnp.int32)
            q_stack = jnp.concatenate(
                [q_scr[par, kv * PAIRS_PER_KV + n, pl.ds(row0, ATTN_BLOCK), :] for n in range(PAIRS_PER_KV)], axis=0)
            for slot in range(2):
                sink = jnp.concatenate(
                    [jnp.full((ATTN_BLOCK, LANES), sinks_ref[2 * (kv * PAIRS_PER_KV + n) + slot], F32)
                     for n in range(PAIRS_PER_KV)], axis=0)
                s = lax.dot_general(q_stack, kz_scr[par, kv, slot, pl.ds(row0, window), :], (((1,), (1,)), ((), ())),
                                    preferred_element_type=F32) + bias_scr[first]
                mx = jnp.maximum(jnp.broadcast_to(jnp.max(s, axis=-1, keepdims=True), (stack, LANES)), sink)
                s_scr[u % 2, slot] = s
                mx_scr[u % 2, slot] = mx
                es_scr[u % 2, slot] = sink - mx

        def values(u):
            kv, i = divmod(u, n_blk)
            row0 = i * ATTN_BLOCK
            outs = []
            for slot in range(2):
                mx = mx_scr[u % 2, slot]
                p = jnp.exp(s_scr[u % 2, slot] - jnp.concatenate([mx, mx], axis=1)).astype(BF16)
                outs.append(jnp.dot(p, vz_scr[par, kv, slot, pl.ds(row0, window), :], preferred_element_type=F32))
            low = lax.broadcasted_iota(jnp.int32, (1, LANES), 1) < HEAD_DIM
            num = jnp.where(low, outs[0], outs[1])
            sums = pltpu.roll(jnp.where(low, outs[1], outs[0]), HEAD_DIM, axis=1)
            den = sums + jnp.exp(jnp.where(low, es_scr[u % 2, 0], es_scr[u % 2, 1]))
            o = num / den
            for n in range(PAIRS_PER_KV):
                o_scr[kv * PAIRS_PER_KV + n, pl.ds(row0, ATTN_BLOCK), :] = o[n * ATTN_BLOCK:(n + 1) * ATTN_BLOCK]

        scores(0)
        for u in range(1, n_units + 1):
            for f in fillers[(u - 1) * len(fillers) // n_units:u * len(fillers) // n_units]:
                f()
            values(u - 1)
            if u < n_units:
                scores(u)

        gate = proj_scr[par, :, _G0:]
        attn = jnp.concatenate([o_scr[r] for r in range(N_Q_HEADS // 2)], axis=1)
        act = (attn * (gate * _sigmoid(gate))).astype(BF16)
        y = xc_ref[...] + jnp.dot(act, wout_ref[...], preferred_element_type=F32)
        if final_norm:
            y = _rms_scale(y, fin_ref[...])
        o_ref[...] = y

    @pl.when(step == 0)
    def _():
        qi = lax.broadcasted_iota(jnp.int32, (stack, window), 0) % ATTN_BLOCK
        kj = lax.broadcasted_iota(jnp.int32, (stack, window), 1)
        dist = qi + ATTN_BLOCK - kj
        band = (dist >= 0) & (dist < ATTN_BLOCK)
        bias_scr[0] = jnp.where(band, 0.0, NEG_INF)
        bias_scr[1] = jnp.where(band & (kj >= ATTN_BLOCK), 0.0, NEG_INF)

    @pl.when(step == 0)
    def _():
        for f in project_tiles(xc_ref, 0):
            f()
        finish_projection(cosc_ref, sinc_ref, 0, None, 0)

    for par in range(2):
        @pl.when(lax.rem(step, 2) == par)
        def _(par=par):
            attend(par, project_tiles(xn_ref, 1 - par))
            finish_projection(cosn_ref, sinn_ref, 1 - par, par, t_nxt)


def _attn_layer(x_bt, norm, win, sinks, w_out, cos, sin, fin, *, final_norm):
    bsz, seq, dm = x_bt.shape
    rows = min(ATTN_ROWS, seq)
    n_t = seq // rows
    n_steps = bsz * n_t
    stack = PAIRS_PER_KV * ATTN_BLOCK

    def nxt(b, t):
        f = jnp.minimum(b * n_t + t + 1, n_steps - 1)
        return f // n_t, f % n_t

    cur_spec = pl.BlockSpec((None, rows, dm), lambda b, t: (b, t, 0))
    nxt_spec = pl.BlockSpec((None, rows, dm), lambda b, t: (*nxt(b, t), 0))
    tab_cur = pl.BlockSpec((rows, LANES), lambda b, t: (t, 0))
    tab_nxt = pl.BlockSpec((rows, LANES), lambda b, t: (nxt(b, t)[1], 0))
    return pl.pallas_call(
        functools.partial(_attn_layer_kernel, rows=rows, n_t=n_t, n_steps=n_steps, final_norm=final_norm),
        out_shape=jax.ShapeDtypeStruct((bsz, seq, dm), F32),
        grid=(bsz, n_t),
        in_specs=[
            pl.BlockSpec(memory_space=pltpu.SMEM),
            cur_spec,
            nxt_spec,
            _const_spec((1, dm)),
            _const_spec((dm, _PROJ_COLS)),
            tab_cur,
            tab_cur,
            tab_nxt,
            tab_nxt,
            _const_spec((BRANCH, dm)),
            _const_spec((1, dm)),
        ],
        out_specs=cur_spec,
        scratch_shapes=[
            pltpu.VMEM((2, rows, _PROJ_COLS), F32),
            pltpu.VMEM((rows, D_MODEL), BF16),
            pltpu.VMEM((2, N_Q_HEADS // 2, rows, LANES), BF16),
            pltpu.VMEM((2, N_KV_HEADS, 2, ATTN_BLOCK + rows, LANES), BF16),
            pltpu.VMEM((2, N_KV_HEADS, 2, ATTN_BLOCK + rows, LANES), BF16),
            pltpu.VMEM((N_Q_HEADS // 2, rows, LANES), F32),
            pltpu.VMEM((2, stack, 2 * ATTN_BLOCK), F32),
            pltpu.VMEM((2, 2, stack, 2 * ATTN_BLOCK), F32),
            pltpu.VMEM((2, 2, stack, LANES), F32),
            pltpu.VMEM((2, 2, stack, LANES), F32),
        ],
        compiler_params=pltpu.CompilerParams(dimension_semantics=("arbitrary", "arbitrary"), vmem_limit_bytes=VMEM_LIMIT),
        name="attn_layer",
    )(sinks.astype(F32), x_bt, x_bt, norm.astype(F32).reshape(1, dm), win, cos, sin, cos, sin, w_out.astype(BF16),
      fin.astype(F32).reshape(1, dm))


def kernel(x, l0_norm, l0_w_in, l0_a_re, l0_a_im, l0_log_step, l0_b_re, l0_b_im, l0_c_re, l0_c_im, l0_d, l0_w_glu, l0_b_glu, l0_w_out, l1_norm, l1_w_in, l1_sinks, l1_w_out, l2_norm, l2_w_in, l2_a_re, l2_a_im, l2_log_step, l2_b_re, l2_b_im, l2_c_re, l2_c_im, l2_d, l2_w_glu, l2_b_glu, l2_w_out, l3_norm, l3_w_in, l3_sinks, l3_w_out, final_norm):
    bsz, seq, _ = x.shape
    assert bsz == BATCH
    cos, sin = _rope_tables(seq)
    mix0, mix2 = _ssm_weights([(l0_a_re, l0_a_im, l0_log_step, l0_b_re, l0_b_im, l0_c_re, l0_c_im),
                               (l2_a_re, l2_a_im, l2_log_step, l2_b_re, l2_b_im, l2_c_re, l2_c_im)])
    win1, win3 = jax.vmap(_attn_weight_layout)(jnp.stack([l1_w_in, l3_w_in]).astype(F32))
    h = x.astype(F32)
    h = _ssm_layer(h, l0_norm, l0_w_in, mix0, l0_d, l0_w_glu, l0_b_glu, l0_w_out)
    h = _attn_layer(h, l1_norm, win1, l1_sinks, l1_w_out, cos, sin, final_norm, final_norm=False)
    h = _ssm_layer(h, l2_norm, l2_w_in, mix2, l2_d, l2_w_glu, l2_b_glu, l2_w_out)
    h = _attn_layer(h, l3_norm, win3, l3_sinks, l3_w_out, cos, sin, final_norm, final_norm=True)
    return h.astype(x.dtype)
```

```python
import functools
import math

import numpy as np
import jax
import jax.numpy as jnp
from jax import lax
from jax.experimental import pallas as pl
from jax.experimental.pallas import tpu as pltpu

D_MODEL = 1024
BATCH = 8
BRANCH = D_MODEL
SSM_GROUP = 16
SSM_GROUPS = BRANCH // SSM_GROUP
SSM_STATE = 64
HEAD_DIM = 64
N_Q_HEADS = BRANCH // HEAD_DIM
N_KV_HEADS = 2
ATTN_BLOCK = 128
ROPE_THETA = 10000.0
NORM_EPS = 1e-5
NEG_INF = -1e30

LANES = 128
SSM_CHUNK = 4
QUAD_CH = LANES // 2
QUAD_GROUPS = QUAD_CH // SSM_GROUP
N_QUADS = BRANCH // QUAD_CH
QUAD_STATE = QUAD_GROUPS * SSM_STATE
HALF = HEAD_DIM // 2

SSM_T_BLOCK = 128
ATTN_ROWS = 512
VMEM_LIMIT = 56 * 1024 * 1024

F32 = jnp.float32
BF16 = jnp.bfloat16


def _sigmoid(v):
    return 1.0 / (1.0 + jnp.exp(-v))


def _gelu_tanh(v):
    c = math.sqrt(2.0 / math.pi)
    return 0.5 * v * (1.0 + jnp.tanh(c * (v + 0.044715 * (v * v * v))))


def _rms_scale(x, g):
    ms = jnp.mean(x * x, axis=-1, keepdims=True)
    return x * lax.rsqrt(ms + NORM_EPS) * g


def _const_spec(shape):
    nd = len(shape)
    return pl.BlockSpec(shape, lambda *_: (0,) * nd, pipeline_mode=pl.Buffered(1))


def _ssm_prep_kernel(are_ref, aim_ref, ls_ref, bre_ref, bim_ref, abr_ref, abi_ref, bbr_ref, bbi_ref):
    lr = are_ref[...]
    li = aim_ref[...]
    step = jnp.exp(ls_ref[...])
    mag = jnp.exp(lr * step)
    ar = mag * jnp.cos(li * step)
    ai = mag * jnp.sin(li * step)
    xr = ar - 1.0
    den = lr * lr + li * li
    cr = (xr * lr + ai * li) / den
    ci = (ai * lr - xr * li) / den
    br = bre_ref[...]
    bi = bim_ref[...]
    abr_ref[...] = ar
    abi_ref[...] = ai
    bbr_ref[...] = cr * br - ci * bi
    bbi_ref[...] = cr * bi + ci * br


def _ssm_prep(a_re, a_im, log_step, b_re, b_im):
    g, p, c = b_re.shape
    rep = lambda a: jnp.repeat(a.astype(F32), c, axis=1)
    ls = jnp.broadcast_to(log_step.astype(F32)[:, None], (g, p * c))
    shp = jax.ShapeDtypeStruct((g, p * c), F32)
    abr, abi, bbr, bbi = pl.pallas_call(
        _ssm_prep_kernel, out_shape=(shp, shp, shp, shp), name="ssm_prep",
    )(rep(a_re), rep(a_im), ls, b_re.astype(F32).reshape(g, p * c), b_im.astype(F32).reshape(g, p * c))
    a_bar_re = abr.reshape(g, p, c)[:, :, 0]
    a_bar_im = abi.reshape(g, p, c)[:, :, 0]
    return a_bar_re, a_bar_im, bbr.reshape(g, p, c), bbi.reshape(g, p, c)


def _cmul(ar, ai, br, bi):
    return ar * br - ai * bi, ar * bi + ai * br


def _ssm_matrices(a_re, a_im, bb_re, bb_im, c_re, c_im):
    hi_p = lax.Precision.HIGHEST
    r = SSM_CHUNK
    cr = c_re.astype(F32)
    ci = c_im.astype(F32)
    pw = [(jnp.ones_like(a_re), jnp.zeros_like(a_im))]
    for _ in range(r):
        pw.append(_cmul(pw[-1][0], pw[-1][1], a_re, a_im))
    ca = [_cmul(cr, ci, p_r[:, None, :], p_i[:, None, :]) for p_r, p_i in pw]
    ab = [_cmul(p_r[:, :, None], p_i[:, :, None], bb_re, bb_im) for p_r, p_i in pw[:r]]
    kd = [jnp.einsum("gcp,gpd->gcd", ca[k][0], bb_re, precision=hi_p)
          - jnp.einsum("gcp,gpd->gcd", ca[k][1], bb_im, precision=hi_p) for k in range(r)]

    diag = jnp.eye(QUAD_GROUPS, dtype=bool)[None, None, :, None, None, :, None]
    n_quads = a_re.shape[0] // QUAD_GROUPS
    quad_shape = (n_quads, QUAD_GROUPS)
    zero = jnp.zeros_like(kd[0])
    t = jnp.stack([jnp.stack([kd[d - e] if d >= e else zero for d in range(r)]) for e in range(r)])
    t = t.reshape(r, r, *quad_shape, SSM_GROUP, SSM_GROUP).transpose(2, 0, 5, 1, 3, 4)
    w_direct = jnp.where(diag, t[:, :, None], 0.0).reshape(n_quads, r * QUAD_CH, r * QUAD_CH)
    s = jnp.stack([jnp.stack(ab[r - 1 - e]) for e in range(r)])
    s = s.reshape(r, 2, *quad_shape, SSM_STATE, SSM_GROUP).transpose(2, 0, 5, 1, 3, 4)
    w_state = jnp.where(diag, s[:, :, None], 0.0).reshape(n_quads, r * QUAD_CH, 2 * QUAD_STATE)
    m = jnp.stack([jnp.stack([ca[d + 1][0], -ca[d + 1][1]]) for d in range(r)])
    m = m.reshape(r, 2, *quad_shape, SSM_GROUP, SSM_STATE).transpose(2, 1, 3, 5, 0, 4)
    w_carry = jnp.where(diag, m[:, :, :, :, :, None, :], 0.0).reshape(n_quads, 2 * QUAD_STATE, r * QUAD_CH)
    arr8 = jnp.broadcast_to(pw[r][0].reshape(1, -1), (BATCH, a_re.size))
    ari8 = jnp.broadcast_to(pw[r][1].reshape(1, -1), (BATCH, a_re.size))
    return w_direct.astype(BF16), w_state.astype(BF16), w_carry.astype(BF16), arr8, ari8


def _ssm_weights(layers):
    cat = [jnp.concatenate([lay[k].astype(F32) for lay in layers], axis=0) for k in range(7)]
    a_bar_re, a_bar_im, bb_re, bb_im = _ssm_prep(*cat[:5])
    wd, ws, wc, arr8, ari8 = _ssm_matrices(a_bar_re, a_bar_im, bb_re, bb_im, cat[5], cat[6])
    n_state = SSM_GROUPS * SSM_STATE
    return [(wd[n * N_QUADS:(n + 1) * N_QUADS], ws[n * N_QUADS:(n + 1) * N_QUADS], wc[n * N_QUADS:(n + 1) * N_QUADS],
             arr8[:, n * n_state:(n + 1) * n_state], ari8[:, n * n_state:(n + 1) * n_state])
            for n in range(len(layers))]


def _ssm_layer_kernel(x_hbm, nrm_ref, win_ref, wd_ref, ws_ref, wc_ref, a2r_ref, a2i_ref, d_ref, wglu_ref, bglu_ref,
                      wout_ref, o_hbm, xin_scr, xout_scr, in_sem, out_sem, h_scr, proj_scr, hb_scr, y_scr,
                      *, t_block, n_steps):
    i = pl.program_id(0)
    slot = i % 2
    tm = t_block * BATCH
    n_chunks = t_block // SSM_CHUNK
    cr = n_chunks * BATCH
    n_cols = BRANCH // LANES

    def in_copy(step, buf, b):
        return pltpu.make_async_copy(x_hbm.at[b, pl.ds(step * t_block, t_block), :], xin_scr.at[buf, :, b, :],
                                     in_sem.at[buf, b])

    def out_copy(step, buf, b):
        return pltpu.make_async_copy(xout_scr.at[buf, :, b, :], o_hbm.at[b, pl.ds(step * t_block, t_block), :],
                                     out_sem.at[buf, b])

    @pl.when(i == 0)
    def _():
        h_scr[...] = jnp.zeros_like(h_scr)
        for b in range(BATCH):
            in_copy(0, 0, b).start()

    @pl.when(i + 1 < n_steps)
    def _():
        for b in range(BATCH):
            in_copy(i + 1, 1 - slot, b).start()

    for b in range(BATCH):
        in_copy(i, slot, b).wait()

    x = xin_scr[slot].reshape(tm, D_MODEL)
    hn = _rms_scale(x, nrm_ref[...]).astype(BF16)
    proj_scr[:, :, :, :BRANCH] = jnp.dot(hn, win_ref[:, :BRANCH], preferred_element_type=F32).reshape(
        n_chunks, SSM_CHUNK, BATCH, BRANCH)

    def gate_tile(k):
        cols = slice(BRANCH + k * PROJ_TILE, BRANCH + (k + 1) * PROJ_TILE)
        proj_scr[:, :, :, cols] = jnp.dot(hn, win_ref[:, cols], preferred_element_type=F32).reshape(
            n_chunks, SSM_CHUNK, BATCH, PROJ_TILE)

    gate_tiles = [functools.partial(gate_tile, k) for k in range(BRANCH // PROJ_TILE)]

    low = lax.broadcasted_iota(jnp.int32, (1, LANES), 1) < QUAD_CH
    for m in range(n_cols):
        ch = slice(m * LANES, (m + 1) * LANES)
        src = [proj_scr[:, e, :, ch].reshape(cr, LANES) for e in range(SSM_CHUNK)]
        swp = [pltpu.roll(v, QUAD_CH, axis=1) for v in src]
        ups = (jnp.concatenate([jnp.where(low, src[0], swp[1]), jnp.where(low, src[2], swp[3])], axis=1),
               jnp.concatenate([jnp.where(low, swp[0], src[1]), jnp.where(low, swp[2], src[3])], axis=1))
        yps = []
        for half, up in enumerate(ups):
            q = 2 * m + half
            st = slice(q * 2 * QUAD_STATE, (q + 1) * 2 * QUAD_STATE)
            hb = hb_scr.at[m % 2, half]
            up = up.astype(BF16)
            yp = jnp.dot(up, wd_ref[q], preferred_element_type=F32)
            hb[BATCH:, :] = jnp.dot(up, ws_ref[q], preferred_element_type=F32)
            hb[:BATCH, :] = h_scr[:, st]
            arr = a2r_ref[:, q * QUAD_STATE:(q + 1) * QUAD_STATE]
            ari = a2i_ref[:, q * QUAD_STATE:(q + 1) * QUAD_STATE]
            hr = hb[:BATCH, :QUAD_STATE]
            hi = hb[:BATCH, QUAD_STATE:]
            for k in range(n_chunks):
                rows = slice((k + 1) * BATCH, (k + 2) * BATCH)
                hr, hi = (arr * hr - ari * hi + hb[rows, :QUAD_STATE], ari * hr + arr * hi + hb[rows, QUAD_STATE:])
                hb[rows, :QUAD_STATE] = hr
                hb[rows, QUAD_STATE:] = hi
            h_scr[:, st] = hb[cr:, :]
            yps.append(yp + jnp.dot(hb[:cr, :].astype(BF16), wc_ref[q], preferred_element_type=F32))
        for n in range(SSM_CHUNK // 2):
            y0 = yps[0][:, n * LANES:(n + 1) * LANES]
            y1 = yps[1][:, n * LANES:(n + 1) * LANES]
            y_scr[:, 2 * n, :, ch] = jnp.where(low, y0, pltpu.roll(y1, QUAD_CH, axis=1)).reshape(n_chunks, BATCH, LANES)
            y_scr[:, 2 * n + 1, :, ch] = jnp.where(low, pltpu.roll(y0, QUAD_CH, axis=1), y1).reshape(n_chunks, BATCH, LANES)
        for f in gate_tiles[m * len(gate_tiles) // n_cols:(m + 1) * len(gate_tiles) // n_cols]:
            f()

    u = proj_scr[:, :, :, :BRANCH].reshape(tm, BRANCH)
    z = _gelu_tanh(y_scr[...].reshape(tm, BRANCH) + d_ref[...] * u)
    glu = jnp.dot(z.astype(BF16), wglu_ref[...], preferred_element_type=F32) + bglu_ref[...]
    z = z * _sigmoid(glu)
    gate = proj_scr[:, :, :, BRANCH:].reshape(tm, BRANCH)
    act = (z * (gate * _sigmoid(gate))).astype(BF16)
    out = x + jnp.dot(act, wout_ref[...], preferred_element_type=F32)

    @pl.when(i >= 2)
    def _():
        for b in range(BATCH):
            out_copy(i - 2, slot, b).wait()

    xout_scr[slot] = out.reshape(t_block, BATCH, D_MODEL)
    for b in range(BATCH):
        out_copy(i, slot, b).start()

    @pl.when(i == n_steps - 1)
    def _():
        if n_steps >= 2:
            for b in range(BATCH):
                out_copy(i - 1, 1 - slot, b).wait()
        for b in range(BATCH):
            out_copy(i, slot, b).wait()


def _ssm_layer(x_bt, norm, w_in, mixer, d, w_glu, b_glu, w_out):
    bsz, seq, dm = x_bt.shape
    t_block = min(SSM_T_BLOCK, seq)
    n_steps = seq // t_block
    n_chunks = t_block // SSM_CHUNK
    wd, ws, wc, a2r8, a2i8 = mixer
    n_state = SSM_GROUPS * SSM_STATE
    return pl.pallas_call(
        functools.partial(_ssm_layer_kernel, t_block=t_block, n_steps=n_steps),
        out_shape=jax.ShapeDtypeStruct((bsz, seq, dm), F32),
        grid=(n_steps,),
        in_specs=[
            pl.BlockSpec(memory_space=pl.ANY),
            _const_spec((1, dm)),
            _const_spec((dm, 2 * BRANCH)),
            _const_spec(wd.shape),
            _const_spec(ws.shape),
            _const_spec(wc.shape),
            _const_spec((BATCH, n_state)),
            _const_spec((BATCH, n_state)),
            _const_spec((1, BRANCH)),
            _const_spec((BRANCH, BRANCH)),
            _const_spec((1, BRANCH)),
            _const_spec((BRANCH, dm)),
        ],
        out_specs=pl.BlockSpec(memory_space=pl.ANY),
        scratch_shapes=[
            pltpu.VMEM((2, t_block, BATCH, dm), F32),
            pltpu.VMEM((2, t_block, BATCH, dm), F32),
            pltpu.SemaphoreType.DMA((2, BATCH)),
            pltpu.SemaphoreType.DMA((2, BATCH)),
            pltpu.VMEM((BATCH, 2 * n_state), F32),
            pltpu.VMEM((n_chunks, SSM_CHUNK, BATCH, 2 * BRANCH), F32),
            pltpu.VMEM((2, 2, (n_chunks + 1) * BATCH, 2 * QUAD_STATE), F32),
            pltpu.VMEM((n_chunks, SSM_CHUNK, BATCH, BRANCH), F32),
        ],
        compiler_params=pltpu.CompilerParams(dimension_semantics=("arbitrary",), vmem_limit_bytes=VMEM_LIMIT),
        name="ssm_layer",
    )(x_bt, norm.astype(F32).reshape(1, dm), w_in.astype(BF16), wd, ws, wc, a2r8, a2i8,
      d.astype(F32).reshape(1, BRANCH), w_glu.astype(BF16), b_glu.astype(F32).reshape(1, BRANCH), w_out.astype(BF16))


def _rope_kernel(freq_ref, cos_ref, sin_ref):
    rows = cos_ref.shape[0]
    pos = (lax.broadcasted_iota(jnp.int32, cos_ref.shape, 0) + pl.program_id(0) * rows).astype(F32)
    ang = pos * freq_ref[...]
    lane = lax.broadcasted_iota(jnp.int32, cos_ref.shape, 1)
    sin = jnp.sin(ang)
    cos_ref[...] = jnp.cos(ang)
    sin_ref[...] = jnp.where(lane < LANES // 2, -sin, sin)


def _rope_tables(seq):
    inv_freq = (np.float32(ROPE_THETA) ** (-np.arange(0, HEAD_DIM, 2, dtype=np.float32) / np.float32(HEAD_DIM))).astype(np.float32)
    freq = jnp.asarray(np.tile(inv_freq, LANES // HALF).reshape(1, LANES))
    blk = min(1024, seq)
    spec = pl.BlockSpec((blk, LANES), lambda i: (i, 0))
    shp = jax.ShapeDtypeStruct((seq, LANES), F32)
    return pl.pallas_call(
        _rope_kernel, out_shape=(shp, shp), grid=(seq // blk,),
        in_specs=[pl.BlockSpec((1, LANES), lambda i: (0, 0))], out_specs=(spec, spec), name="rope_tables",
    )(freq)


def _attn_weight_layout(w_in):
    wq = w_in[:, :BRANCH] * (HEAD_DIM ** -0.5)
    wk = w_in[:, BRANCH:BRANCH + N_KV_HEADS * HEAD_DIM]
    wv = w_in[:, BRANCH + N_KV_HEADS * HEAD_DIM:BRANCH + 2 * N_KV_HEADS * HEAD_DIM]
    wg = w_in[:, BRANCH + 2 * N_KV_HEADS * HEAD_DIM:]
    dm = w_in.shape[0]
    wq = wq.reshape(dm, N_Q_HEADS // 2, 2, 2, HALF).transpose(0, 1, 3, 2, 4).reshape(dm, BRANCH)
    wk = jnp.broadcast_to(wk.reshape(dm, N_KV_HEADS, 2, 1, HALF), (dm, N_KV_HEADS, 2, 2, HALF))
    wk = wk.reshape(dm, N_KV_HEADS * LANES)
    wv = jnp.broadcast_to(wv.reshape(dm, N_KV_HEADS, 1, HEAD_DIM), (dm, N_KV_HEADS, 2, HEAD_DIM))
    wv = wv.reshape(dm, N_KV_HEADS * LANES)
    return jnp.concatenate([wq, wk, wv, wg], axis=1).astype(BF16)


_Q0 = 0
_K0 = BRANCH
_V0 = BRANCH + N_KV_HEADS * LANES
_G0 = BRANCH + 2 * N_KV_HEADS * LANES
_PROJ_COLS = _G0 + BRANCH
PAIRS_PER_KV = N_Q_HEADS // 2 // N_KV_HEADS
PROJ_TILE = 256


def _attn_layer_kernel(sinks_ref, xc_ref, xn_ref, nrm_ref, win_ref, cosc_ref, sinc_ref, cosn_ref, sinn_ref, wout_ref,
                       fin_ref, o_ref,
                       proj_scr, hn_scr, q_scr, kz_scr, vz_scr, o_scr, bias_scr, s_scr, mx_scr, es_scr,
                       *, rows, n_t, n_steps, final_norm):
    step = pl.program_id(0) * n_t + pl.program_id(1)
    ti = pl.program_id(1)
    nxt = jnp.minimum(step + 1, n_steps - 1)
    t_nxt = lax.rem(nxt, n_t)
    n_blk = rows // ATTN_BLOCK
    n_units = N_KV_HEADS * n_blk
    stack = PAIRS_PER_KV * ATTN_BLOCK
    window = 2 * ATTN_BLOCK

    n_tiles = _PROJ_COLS // PROJ_TILE

    def project_tiles(x_ref, par):
        def norm_and_tile(k):
            if k == 0:
                hn_scr[...] = _rms_scale(x_ref[...], nrm_ref[...]).astype(BF16)
            cols = slice(k * PROJ_TILE, (k + 1) * PROJ_TILE)
            proj_scr[par, :, cols] = jnp.dot(hn_scr[...], win_ref[:, cols], preferred_element_type=F32)
        return [functools.partial(norm_and_tile, k) for k in range(n_tiles)]

    def finish_projection(cos_ref, sin_ref, par, prev, t_blk):
        cos = cos_ref[...]
        sin = sin_ref[...]

        def rope(t):
            return t * cos + pltpu.roll(t, LANES // 2, axis=1) * sin

        for r in range(N_Q_HEADS // 2):
            q_scr[par, r] = rope(proj_scr[par, :, _Q0 + r * LANES:_Q0 + (r + 1) * LANES]).astype(BF16)
        keep = t_blk > 0
        if prev is None:
            kz_scr[par, :, :, :ATTN_BLOCK, :] = jnp.zeros((N_KV_HEADS, 2, ATTN_BLOCK, LANES), BF16)
            vz_scr[par, :, :, :ATTN_BLOCK, :] = jnp.zeros((N_KV_HEADS, 2, ATTN_BLOCK, LANES), BF16)
        else:
            kz_scr[par, :, :, :ATTN_BLOCK, :] = jnp.where(keep, kz_scr[prev, :, :, rows:, :], 0.0).astype(BF16)
            vz_scr[par, :, :, :ATTN_BLOCK, :] = jnp.where(keep, vz_scr[prev, :, :, rows:, :], 0.0).astype(BF16)
        lane = lax.broadcasted_iota(jnp.int32, (1, LANES), 1)
        for kv in range(N_KV_HEADS):
            k_rep = rope(proj_scr[par, :, _K0 + kv * LANES:_K0 + (kv + 1) * LANES])
            v_rep = proj_scr[par, :, _V0 + kv * LANES:_V0 + (kv + 1) * LANES]
            for slot in range(2):
                k_mask = (lane // HALF) % 2 == slot
                v_mask = lane // HEAD_DIM == slot
                kz_scr[par, kv, slot, ATTN_BLOCK:, :] = jnp.where(k_mask, k_rep, 0.0).astype(BF16)
                vz_scr[par, kv, slot, ATTN_BLOCK:, :] = jnp.where(v_mask, v_rep, 1.0).astype(BF16)

    def attend(par, fillers):
        def scores(u):
            kv, i = divmod(u, n_blk)
            row0 = i * ATTN_BLOCK
            first = jnp.logical_and(ti == 0, i == 0).astype(jnp.int32)
            q_stack = jnp.concatenate(
                [q_scr[par, kv * PAIRS_PER_KV + n, pl.ds(row0, ATTN_BLOCK), :] for n in range(PAIRS_PER_KV)], axis=0)
            for slot in range(2):
                sink = jnp.concatenate(
                    [jnp.full((ATTN_BLOCK, LANES), sinks_ref[2 * (kv * PAIRS_PER_KV + n) + slot], F32)
                     for n in range(PAIRS_PER_KV)], axis=0)
                s = lax.dot_general(q_stack, kz_scr[par, kv, slot, pl.ds(row0, window), :], (((1,), (1,)), ((), ())),
                                    preferred_element_type=F32) + bias_scr[first]
                mx = jnp.maximum(jnp.broadcast_to(jnp.max(s, axis=-1, keepdims=True), (stack, LANES)), sink)
                s_scr[u % 2, slot] = s
                mx_scr[u % 2, slot] = mx
                es_scr[u % 2, slot] = sink - mx

        def values(u):
            kv, i = divmod(u, n_blk)
            row0 = i * ATTN_BLOCK
            outs = []
            for slot in range(2):
                mx = mx_scr[u % 2, slot]
                p = jnp.exp(s_scr[u % 2, slot] - jnp.concatenate([mx, mx], axis=1)).astype(BF16)
                outs.append(jnp.dot(p, vz_scr[par, kv, slot, pl.ds(row0, window), :], preferred_element_type=F32))
            low = lax.broadcasted_iota(jnp.int32, (1, LANES), 1) < HEAD_DIM
            num = jnp.where(low, outs[0], outs[1])
            sums = pltpu.roll(jnp.where(low, outs[1], outs[0]), HEAD_DIM, axis=1)
            den = sums + jnp.exp(jnp.where(low, es_scr[u % 2, 0], es_scr[u % 2, 1]))
            o = num / den
            for n in range(PAIRS_PER_KV):
                o_scr[kv * PAIRS_PER_KV + n, pl.ds(row0, ATTN_BLOCK), :] = o[n * ATTN_BLOCK:(n + 1) * ATTN_BLOCK]

        scores(0)
        for u in range(1, n_units + 1):
            for f in fillers[(u - 1) * len(fillers) // n_units:u * len(fillers) // n_units]:
                f()
            values(u - 1)
            if u < n_units:
                scores(u)

        gate = proj_scr[par, :, _G0:]
        attn = jnp.concatenate([o_scr[r] for r in range(N_Q_HEADS // 2)], axis=1)
        act = (attn * (gate * _sigmoid(gate))).astype(BF16)
        y = xc_ref[...] + jnp.dot(act, wout_ref[...], preferred_element_type=F32)
        if final_norm:
            y = _rms_scale(y, fin_ref[...])
        o_ref[...] = y

    @pl.when(step == 0)
    def _():
        qi = lax.broadcasted_iota(jnp.int32, (stack, window), 0) % ATTN_BLOCK
        kj = lax.broadcasted_iota(jnp.int32, (stack, window), 1)
        dist = qi + ATTN_BLOCK - kj
        band = (dist >= 0) & (dist < ATTN_BLOCK)
        bias_scr[0] = jnp.where(band, 0.0, NEG_INF)
        bias_scr[1] = jnp.where(band & (kj >= ATTN_BLOCK), 0.0, NEG_INF)

    @pl.when(step == 0)
    def _():
        for f in project_tiles(xc_ref, 0):
            f()
        finish_projection(cosc_ref, sinc_ref, 0, None, 0)

    for par in range(2):
        @pl.when(lax.rem(step, 2) == par)
        def _(par=par):
            attend(par, project_tiles(xn_ref, 1 - par))
            finish_projection(cosn_ref, sinn_ref, 1 - par, par, t_nxt)


def _attn_layer(x_bt, norm, win, sinks, w_out, cos, sin, fin, *, final_norm):
    bsz, seq, dm = x_bt.shape
    rows = min(ATTN_ROWS, seq)
    n_t = seq // rows
    n_steps = bsz * n_t
    stack = PAIRS_PER_KV * ATTN_BLOCK

    def nxt(b, t):
        f = jnp.minimum(b * n_t + t + 1, n_steps - 1)
        return f // n_t, f % n_t

    cur_spec = pl.BlockSpec((None, rows, dm), lambda b, t: (b, t, 0))
    nxt_spec = pl.BlockSpec((None, rows, dm), lambda b, t: (*nxt(b, t), 0))
    tab_cur = pl.BlockSpec((rows, LANES), lambda b, t: (t, 0))
    tab_nxt = pl.BlockSpec((rows, LANES), lambda b, t: (nxt(b, t)[1], 0))
    return pl.pallas_call(
        functools.partial(_attn_layer_kernel, rows=rows, n_t=n_t, n_steps=n_steps, final_norm=final_norm),
        out_shape=jax.ShapeDtypeStruct((bsz, seq, dm), F32),
        grid=(bsz, n_t),
        in_specs=[
            pl.BlockSpec(memory_space=pltpu.SMEM),
            cur_spec,
            nxt_spec,
            _const_spec((1, dm)),
            _const_spec((dm, _PROJ_COLS)),
            tab_cur,
            tab_cur,
            tab_nxt,
            tab_nxt,
            _const_spec((BRANCH, dm)),
            _const_spec((1, dm)),
        ],
        out_specs=cur_spec,
        scratch_shapes=[
            pltpu.VMEM((2, rows, _PROJ_COLS), F32),
            pltpu.VMEM((rows, D_MODEL), BF16),
            pltpu.VMEM((2, N_Q_HEADS // 2, rows, LANES), BF16),
            pltpu.VMEM((2, N_KV_HEADS, 2, ATTN_BLOCK + rows, LANES), BF16),
            pltpu.VMEM((2, N_KV_HEADS, 2, ATTN_BLOCK + rows, LANES), BF16),
            pltpu.VMEM((N_Q_HEADS // 2, rows, LANES), F32),
            pltpu.VMEM((2, stack, 2 * ATTN_BLOCK), F32),
            pltpu.VMEM((2, 2, stack, 2 * ATTN_BLOCK), F32),
            pltpu.VMEM((2, 2, stack, LANES), F32),
            pltpu.VMEM((2, 2, stack, LANES), F32),
        ],
        compiler_params=pltpu.CompilerParams(dimension_semantics=("arbitrary", "arbitrary"), vmem_limit_bytes=VMEM_LIMIT),
        name="attn_layer",
    )(sinks.astype(F32), x_bt, x_bt, norm.astype(F32).reshape(1, dm), win, cos, sin, cos, sin, w_out.astype(BF16),
      fin.astype(F32).reshape(1, dm))


def kernel(x, l0_norm, l0_w_in, l0_a_re, l0_a_im, l0_log_step, l0_b_re, l0_b_im, l0_c_re, l0_c_im, l0_d, l0_w_glu, l0_b_glu, l0_w_out, l1_norm, l1_w_in, l1_sinks, l1_w_out, l2_norm, l2_w_in, l2_a_re, l2_a_im, l2_log_step, l2_b_re, l2_b_im, l2_c_re, l2_c_im, l2_d, l2_w_glu, l2_b_glu, l2_w_out, l3_norm, l3_w_in, l3_sinks, l3_w_out, final_norm):
    bsz, seq, _ = x.shape
    assert bsz == BATCH
    cos, sin = _rope_tables(seq)
    mix0, mix2 = _ssm_weights([(l0_a_re, l0_a_im, l0_log_step, l0_b_re, l0_b_im, l0_c_re, l0_c_im),
                               (l2_a_re, l2_a_im, l2_log_step, l2_b_re, l2_b_im, l2_c_re, l2_c_im)])
    win1, win3 = jax.vmap(_attn_weight_layout)(jnp.stack([l1_w_in, l3_w_in]).astype(F32))
    h = x.astype(F32)
    h = _ssm_layer(h, l0_norm, l0_w_in, mix0, l0_d, l0_w_glu, l0_b_glu, l0_w_out)
    h = _attn_layer(h, l1_norm, win1, l1_sinks, l1_w_out, cos, sin, final_norm, final_norm=False)
    h = _ssm_layer(h, l2_norm, l2_w_in, mix2, l2_d, l2_w_glu, l2_b_glu, l2_w_out)
    h = _attn_layer(h, l3_norm, win3, l3_sinks, l3_w_out, cos, sin, final_norm, final_norm=True)
    return h.astype(x.dtype)
```

```python
import functools
import math

import numpy as np
import jax
import jax.numpy as jnp
from jax import lax
from jax.experimental import pallas as pl
from jax.experimental.pallas import tpu as pltpu

D_MODEL = 1024
BATCH = 8
BRANCH = D_MODEL
SSM_GROUP = 16
SSM_GROUPS = BRANCH // SSM_GROUP
SSM_STATE = 64
HEAD_DIM = 64
N_Q_HEADS = BRANCH // HEAD_DIM
N_KV_HEADS = 2
ATTN_BLOCK = 128
ROPE_THETA = 10000.0
NORM_EPS = 1e-5
NEG_INF = -1e30

LANES = 128
SSM_CHUNK = 4
QUAD_CH = LANES // 2
QUAD_GROUPS = QUAD_CH // SSM_GROUP
N_QUADS = BRANCH // QUAD_CH
QUAD_STATE = QUAD_GROUPS * SSM_STATE
HALF = HEAD_DIM // 2

SSM_T_BLOCK = 128
ATTN_ROWS = 512
VMEM_LIMIT = 56 * 1024 * 1024

F32 = jnp.float32
BF16 = jnp.bfloat16


def _sigmoid(v):
    return 1.0 / (1.0 + jnp.exp(-v))


def _gelu_tanh(v):
    c = math.sqrt(2.0 / math.pi)
    return 0.5 * v * (1.0 + jnp.tanh(c * (v + 0.044715 * (v * v * v))))


def _rms_scale(x, g):
    ms = jnp.mean(x * x, axis=-1, keepdims=True)
    return x * lax.rsqrt(ms + NORM_EPS) * g


def _const_spec(shape):
    nd = len(shape)
    return pl.BlockSpec(shape, lambda *_: (0,) * nd, pipeline_mode=pl.Buffered(1))


def _ssm_prep_kernel(are_ref, aim_ref, ls_ref, bre_ref, bim_ref, abr_ref, abi_ref, bbr_ref, bbi_ref):
    lr = are_ref[...]
    li = aim_ref[...]
    step = jnp.exp(ls_ref[...])
    mag = jnp.exp(lr * step)
    ar = mag * jnp.cos(li * step)
    ai = mag * jnp.sin(li * step)
    xr = ar - 1.0
    den = lr * lr + li * li
    cr = (xr * lr + ai * li) / den
    ci = (ai * lr - xr * li) / den
    br = bre_ref[...]
    bi = bim_ref[...]
    abr_ref[...] = ar
    abi_ref[...] = ai
    bbr_ref[...] = cr * br - ci * bi
    bbi_ref[...] = cr * bi + ci * br


def _ssm_prep(a_re, a_im, log_step, b_re, b_im):
    g, p, c = b_re.shape
    rep = lambda a: jnp.repeat(a.astype(F32), c, axis=1)
    ls = jnp.broadcast_to(log_step.astype(F32)[:, None], (g, p * c))
    shp = jax.ShapeDtypeStruct((g, p * c), F32)
    abr, abi, bbr, bbi = pl.pallas_call(
        _ssm_prep_kernel, out_shape=(shp, shp, shp, shp), name="ssm_prep",
    )(rep(a_re), rep(a_im), ls, b_re.astype(F32).reshape(g, p * c), b_im.astype(F32).reshape(g, p * c))
    a_bar_re = abr.reshape(g, p, c)[:, :, 0]
    a_bar_im = abi.reshape(g, p, c)[:, :, 0]
    return a_bar_re, a_bar_im, bbr.reshape(g, p, c), bbi.reshape(g, p, c)


def _cmul(ar, ai, br, bi):
    return ar * br - ai * bi, ar * bi + ai * br


def _ssm_matrices(a_re, a_im, bb_re, bb_im, c_re, c_im):
    hi_p = lax.Precision.HIGHEST
    r = SSM_CHUNK
    cr = c_re.astype(F32)
    ci = c_im.astype(F32)
    pw = [(jnp.ones_like(a_re), jnp.zeros_like(a_im))]
    for _ in range(r):
        pw.append(_cmul(pw[-1][0], pw[-1][1], a_re, a_im))
    ca = [_cmul(cr, ci, p_r[:, None, :], p_i[:, None, :]) for p_r, p_i in pw]
    ab = [_cmul(p_r[:, :, None], p_i[:, :, None], bb_re, bb_im) for p_r, p_i in pw[:r]]
    kd = [jnp.einsum("gcp,gpd->gcd", ca[k][0], bb_re, precision=hi_p)
          - jnp.einsum("gcp,gpd->gcd", ca[k][1], bb_im, precision=hi_p) for k in range(r)]

    diag = jnp.eye(QUAD_GROUPS, dtype=bool)[None, None, :, None, None, :, None]
    n_quads = a_re.shape[0] // QUAD_GROUPS
    quad_shape = (n_quads, QUAD_GROUPS)
    zero = jnp.zeros_like(kd[0])
    t = jnp.stack([jnp.stack([kd[d - e] if d >= e else zero for d in range(r)]) for e in range(r)])
    t = t.reshape(r, r, *quad_shape, SSM_GROUP, SSM_GROUP).transpose(2, 0, 5, 1, 3, 4)
    w_direct = jnp.where(diag, t[:, :, None], 0.0).reshape(n_quads, r * QUAD_CH, r * QUAD_CH)
    s = jnp.stack([jnp.stack(ab[r - 1 - e]) for e in range(r)])
    s = s.reshape(r, 2, *quad_shape, SSM_STATE, SSM_GROUP).transpose(2, 0, 5, 1, 3, 4)
    w_state = jnp.where(diag, s[:, :, None], 0.0).reshape(n_quads, r * QUAD_CH, 2 * QUAD_STATE)
    m = jnp.stack([jnp.stack([ca[d + 1][0], -ca[d + 1][1]]) for d in range(r)])
    m = m.reshape(r, 2, *quad_shape, SSM_GROUP, SSM_STATE).transpose(2, 1, 3, 5, 0, 4)
    w_carry = jnp.where(diag, m[:, :, :, :, :, None, :], 0.0).reshape(n_quads, 2 * QUAD_STATE, r * QUAD_CH)
    arr8 = jnp.broadcast_to(pw[r][0].reshape(1, -1), (BATCH, a_re.size))
    ari8 = jnp.broadcast_to(pw[r][1].reshape(1, -1), (BATCH, a_re.size))
    return w_direct.astype(BF16), w_state.astype(BF16), w_carry.astype(BF16), arr8, ari8


def _ssm_weights(layers):
    cat = [jnp.concatenate([lay[k].astype(F32) for lay in layers], axis=0) for k in range(7)]
    a_bar_re, a_bar_im, bb_re, bb_im = _ssm_prep(*cat[:5])
    wd, ws, wc, arr8, ari8 = _ssm_matrices(a_bar_re, a_bar_im, bb_re, bb_im, cat[5], cat[6])
    n_state = SSM_GROUPS * SSM_STATE
    return [(wd[n * N_QUADS:(n + 1) * N_QUADS], ws[n * N_QUADS:(n + 1) * N_QUADS], wc[n * N_QUADS:(n + 1) * N_QUADS],
             arr8[:, n * n_state:(n + 1) * n_state], ari8[:, n * n_state:(n + 1) * n_state])
            for n in range(len(layers))]


def _ssm_layer_kernel(x_hbm, nrm_ref, win_ref, wd_ref, ws_ref, wc_ref, a2r_ref, a2i_ref, d_ref, wglu_ref, bglu_ref,
                      wout_ref, o_hbm, xin_scr, xout_scr, in_sem, out_sem, h_scr, proj_scr, hb_scr, y_scr,
                      *, t_block, n_steps):
    i = pl.program_id(0)
    slot = i % 2
    tm = t_block * BATCH
    n_chunks = t_block // SSM_CHUNK
    cr = n_chunks * BATCH
    n_cols = BRANCH // LANES

    def in_copy(step, buf, b):
        return pltpu.make_async_copy(x_hbm.at[b, pl.ds(step * t_block, t_block), :], xin_scr.at[buf, :, b, :],
                                     in_sem.at[buf, b])

    def out_copy(step, buf, b):
        return pltpu.make_async_copy(xout_scr.at[buf, :, b, :], o_hbm.at[b, pl.ds(step * t_block, t_block), :],
                                     out_sem.at[buf, b])

    @pl.when(i == 0)
    def _():
        h_scr[...] = jnp.zeros_like(h_scr)
        for b in range(BATCH):
            in_copy(0, 0, b).start()

    @pl.when(i + 1 < n_steps)
    def _():
        for b in range(BATCH):
            in_copy(i + 1, 1 - slot, b).start()

    for b in range(BATCH):
        in_copy(i, slot, b).wait()

    x = xin_scr[slot].reshape(tm, D_MODEL)
    hn = _rms_scale(x, nrm_ref[...]).astype(BF16)
    def proj_tile(k):
        cols = slice(k * PROJ_TILE, (k + 1) * PROJ_TILE)
        proj_scr[:, :, :, cols] = jnp.dot(hn, win_ref[:, cols], preferred_element_type=F32).reshape(
            n_chunks, SSM_CHUNK, BATCH, PROJ_TILE)

    n_u_tiles = BRANCH // PROJ_TILE
    cols_per_tile = PROJ_TILE // LANES
    gate_tiles = [functools.partial(proj_tile, n_u_tiles + k) for k in range(n_u_tiles)]
    proj_tile(0)

    low = lax.broadcasted_iota(jnp.int32, (1, LANES), 1) < QUAD_CH
    for m in range(n_cols):
        ch = slice(m * LANES, (m + 1) * LANES)
        src = [proj_scr[:, e, :, ch].reshape(cr, LANES) for e in range(SSM_CHUNK)]
        swp = [pltpu.roll(v, QUAD_CH, axis=1) for v in src]
        ups = (jnp.concatenate([jnp.where(low, src[0], swp[1]), jnp.where(low, src[2], swp[3])], axis=1),
               jnp.concatenate([jnp.where(low, swp[0], src[1]), jnp.where(low, swp[2], src[3])], axis=1))
        yps = []
        for half, up in enumerate(ups):
            q = 2 * m + half
            st = slice(q * 2 * QUAD_STATE, (q + 1) * 2 * QUAD_STATE)
            hb = hb_scr.at[m % 2, half]
            up = up.astype(BF16)
            yp = jnp.dot(up, wd_ref[q], preferred_element_type=F32)
            hb[BATCH:, :] = jnp.dot(up, ws_ref[q], preferred_element_type=F32)
            hb[:BATCH, :] = h_scr[:, st]
            arr = a2r_ref[:, q * QUAD_STATE:(q + 1) * QUAD_STATE]
            ari = a2i_ref[:, q * QUAD_STATE:(q + 1) * QUAD_STATE]
            hr = hb[:BATCH, :QUAD_STATE]
            hi = hb[:BATCH, QUAD_STATE:]
            for k in range(n_chunks):
                rows = slice((k + 1) * BATCH, (k + 2) * BATCH)
                hr, hi = (arr * hr - ari * hi + hb[rows, :QUAD_STATE], ari * hr + arr * hi + hb[rows, QUAD_STATE:])
                hb[rows, :QUAD_STATE] = hr
                hb[rows, QUAD_STATE:] = hi
            h_scr[:, st] = hb[cr:, :]
            yps.append(yp + jnp.dot(hb[:cr, :].astype(BF16), wc_ref[q], preferred_element_type=F32))
        for n in range(SSM_CHUNK // 2):
            y0 = yps[0][:, n * LANES:(n + 1) * LANES]
            y1 = yps[1][:, n * LANES:(n + 1) * LANES]
            y_scr[:, 2 * n, :, ch] = jnp.where(low, y0, pltpu.roll(y1, QUAD_CH, axis=1)).reshape(n_chunks, BATCH, LANES)
            y_scr[:, 2 * n + 1, :, ch] = jnp.where(low, pltpu.roll(y0, QUAD_CH, axis=1), y1).reshape(n_chunks, BATCH, LANES)
        if m % cols_per_tile == 0 and m // cols_per_tile + 1 < n_u_tiles:
            proj_tile(m // cols_per_tile + 1)
        for f in gate_tiles[m * len(gate_tiles) // n_cols:(m + 1) * len(gate_tiles) // n_cols]:
            f()

    u = proj_scr[:, :, :, :BRANCH].reshape(tm, BRANCH)
    z = _gelu_tanh(y_scr[...].reshape(tm, BRANCH) + d_ref[...] * u)
    glu = jnp.dot(z.astype(BF16), wglu_ref[...], preferred_element_type=F32) + bglu_ref[...]
    z = z * _sigmoid(glu)
    gate = proj_scr[:, :, :, BRANCH:].reshape(tm, BRANCH)
    act = (z * (gate * _sigmoid(gate))).astype(BF16)
    out = x + jnp.dot(act, wout_ref[...], preferred_element_type=F32)

    @pl.when(i >= 2)
    def _():
        for b in range(BATCH):
            out_copy(i - 2, slot, b).wait()

    xout_scr[slot] = out.reshape(t_block, BATCH, D_MODEL)
    for b in range(BATCH):
        out_copy(i, slot, b).start()

    @pl.when(i == n_steps - 1)
    def _():
        if n_steps >= 2:
            for b in range(BATCH):
                out_copy(i - 1, 1 - slot, b).wait()
        for b in range(BATCH):
            out_copy(i, slot, b).wait()


def _ssm_layer(x_bt, norm, w_in, mixer, d, w_glu, b_glu, w_out):
    bsz, seq, dm = x_bt.shape
    t_block = min(SSM_T_BLOCK, seq)
    n_steps = seq // t_block
    n_chunks = t_block // SSM_CHUNK
    wd, ws, wc, a2r8, a2i8 = mixer
    n_state = SSM_GROUPS * SSM_STATE
    return pl.pallas_call(
        functools.partial(_ssm_layer_kernel, t_block=t_block, n_steps=n_steps),
        out_shape=jax.ShapeDtypeStruct((bsz, seq, dm), F32),
        grid=(n_steps,),
        in_specs=[
            pl.BlockSpec(memory_space=pl.ANY),
            _const_spec((1, dm)),
            _const_spec((dm, 2 * BRANCH)),
            _const_spec(wd.shape),
            _const_spec(ws.shape),
            _const_spec(wc.shape),
            _const_spec((BATCH, n_state)),
            _const_spec((BATCH, n_state)),
            _const_spec((1, BRANCH)),
            _const_spec((BRANCH, BRANCH)),
            _const_spec((1, BRANCH)),
            _const_spec((BRANCH, dm)),
        ],
        out_specs=pl.BlockSpec(memory_space=pl.ANY),
        scratch_shapes=[
            pltpu.VMEM((2, t_block, BATCH, dm), F32),
            pltpu.VMEM((2, t_block, BATCH, dm), F32),
            pltpu.SemaphoreType.DMA((2, BATCH)),
            pltpu.SemaphoreType.DMA((2, BATCH)),
            pltpu.VMEM((BATCH, 2 * n_state), F32),
            pltpu.VMEM((n_chunks, SSM_CHUNK, BATCH, 2 * BRANCH), F32),
            pltpu.VMEM((2, 2, (n_chunks + 1) * BATCH, 2 * QUAD_STATE), F32),
            pltpu.VMEM((n_chunks, SSM_CHUNK, BATCH, BRANCH), F32),
        ],
        compiler_params=pltpu.CompilerParams(dimension_semantics=("arbitrary",), vmem_limit_bytes=VMEM_LIMIT),
        name="ssm_layer",
    )(x_bt, norm.astype(F32).reshape(1, dm), w_in.astype(BF16), wd, ws, wc, a2r8, a2i8,
      d.astype(F32).reshape(1, BRANCH), w_glu.astype(BF16), b_glu.astype(F32).reshape(1, BRANCH), w_out.astype(BF16))


def _rope_kernel(freq_ref, cos_ref, sin_ref):
    rows = cos_ref.shape[0]
    pos = (lax.broadcasted_iota(jnp.int32, cos_ref.shape, 0) + pl.program_id(0) * rows).astype(F32)
    ang = pos * freq_ref[...]
    lane = lax.broadcasted_iota(jnp.int32, cos_ref.shape, 1)
    sin = jnp.sin(ang)
    cos_ref[...] = jnp.cos(ang)
    sin_ref[...] = jnp.where(lane < LANES // 2, -sin, sin)


def _rope_tables(seq):
    inv_freq = (np.float32(ROPE_THETA) ** (-np.arange(0, HEAD_DIM, 2, dtype=np.float32) / np.float32(HEAD_DIM))).astype(np.float32)
    freq = jnp.asarray(np.tile(inv_freq, LANES // HALF).reshape(1, LANES))
    blk = min(1024, seq)
    spec = pl.BlockSpec((blk, LANES), lambda i: (i, 0))
    shp = jax.ShapeDtypeStruct((seq, LANES), F32)
    return pl.pallas_call(
        _rope_kernel, out_shape=(shp, shp), grid=(seq // blk,),
        in_specs=[pl.BlockSpec((1, LANES), lambda i: (0, 0))], out_specs=(spec, spec), name="rope_tables",
    )(freq)


def _attn_weight_layout(w_in):
    wq = w_in[:, :BRANCH] * (HEAD_DIM ** -0.5)
    wk = w_in[:, BRANCH:BRANCH + N_KV_HEADS * HEAD_DIM]
    wv = w_in[:, BRANCH + N_KV_HEADS * HEAD_DIM:BRANCH + 2 * N_KV_HEADS * HEAD_DIM]
    wg = w_in[:, BRANCH + 2 * N_KV_HEADS * HEAD_DIM:]
    dm = w_in.shape[0]
    wq = wq.reshape(dm, N_Q_HEADS // 2, 2, 2, HALF).transpose(0, 1, 3, 2, 4).reshape(dm, BRANCH)
    wk = jnp.broadcast_to(wk.reshape(dm, N_KV_HEADS, 2, 1, HALF), (dm, N_KV_HEADS, 2, 2, HALF))
    wk = wk.reshape(dm, N_KV_HEADS * LANES)
    wv = jnp.broadcast_to(wv.reshape(dm, N_KV_HEADS, 1, HEAD_DIM), (dm, N_KV_HEADS, 2, HEAD_DIM))
    wv = wv.reshape(dm, N_KV_HEADS * LANES)
    return jnp.concatenate([wq, wk, wv, wg], axis=1).astype(BF16)


_Q0 = 0
_K0 = BRANCH
_V0 = BRANCH + N_KV_HEADS * LANES
_G0 = BRANCH + 2 * N_KV_HEADS * LANES
_PROJ_COLS = _G0 + BRANCH
PAIRS_PER_KV = N_Q_HEADS // 2 // N_KV_HEADS
PROJ_TILE = 256


def _attn_layer_kernel(sinks_ref, xc_ref, xn_ref, nrm_ref, win_ref, cosc_ref, sinc_ref, cosn_ref, sinn_ref, wout_ref,
                       fin_ref, o_ref,
                       proj_scr, hn_scr, q_scr, kz_scr, vz_scr, o_scr, bias_scr, s_scr, mx_scr, es_scr,
                       *, rows, n_t, n_steps, final_norm):
    step = pl.program_id(0) * n_t + pl.program_id(1)
    ti = pl.program_id(1)
    nxt = jnp.minimum(step + 1, n_steps - 1)
    t_nxt = lax.rem(nxt, n_t)
    n_blk = rows // ATTN_BLOCK
    n_units = N_KV_HEADS * n_blk
    stack = PAIRS_PER_KV * ATTN_BLOCK
    window = 2 * ATTN_BLOCK

    n_tiles = _PROJ_COLS // PROJ_TILE

    def project_tiles(x_ref, par):
        def norm_and_tile(k):
            if k == 0:
                hn_scr[...] = _rms_scale(x_ref[...], nrm_ref[...]).astype(BF16)
            cols = slice(k * PROJ_TILE, (k + 1) * PROJ_TILE)
            proj_scr[par, :, cols] = jnp.dot(hn_scr[...], win_ref[:, cols], preferred_element_type=F32)
        return [functools.partial(norm_and_tile, k) for k in range(n_tiles)]

    def finish_projection(cos_ref, sin_ref, par, prev, t_blk):
        cos = cos_ref[...]
        sin = sin_ref[...]

        def rope(t):
            return t * cos + pltpu.roll(t, LANES // 2, axis=1) * sin

        for r in range(N_Q_HEADS // 2):
            q_scr[par, r] = rope(proj_scr[par, :, _Q0 + r * LANES:_Q0 + (r + 1) * LANES]).astype(BF16)
        keep = t_blk > 0
        if prev is None:
            kz_scr[par, :, :, :ATTN_BLOCK, :] = jnp.zeros((N_KV_HEADS, 2, ATTN_BLOCK, LANES), BF16)
            vz_scr[par, :, :, :ATTN_BLOCK, :] = jnp.zeros((N_KV_HEADS, 2, ATTN_BLOCK, LANES), BF16)
        else:
            kz_scr[par, :, :, :ATTN_BLOCK, :] = jnp.where(keep, kz_scr[prev, :, :, rows:, :], 0.0).astype(BF16)
            vz_scr[par, :, :, :ATTN_BLOCK, :] = jnp.where(keep, vz_scr[prev, :, :, rows:, :], 0.0).astype(BF16)
        lane = lax.broadcasted_iota(jnp.int32, (1, LANES), 1)
        for kv in range(N_KV_HEADS):
            k_rep = rope(proj_scr[par, :, _K0 + kv * LANES:_K0 + (kv + 1) * LANES])
            v_rep = proj_scr[par, :, _V0 + kv * LANES:_V0 + (kv + 1) * LANES]
            for slot in range(2):
                k_mask = (lane // HALF) % 2 == slot
                v_mask = lane // HEAD_DIM == slot
                kz_scr[par, kv, slot, ATTN_BLOCK:, :] = jnp.where(k_mask, k_rep, 0.0).astype(BF16)
                vz_scr[par, kv, slot, ATTN_BLOCK:, :] = jnp.where(v_mask, v_rep, 1.0).astype(BF16)

    def attend(par, fillers):
        def scores(u):
            kv, i = divmod(u, n_blk)
            row0 = i * ATTN_BLOCK
            first = jnp.logical_and(ti == 0, i == 0).astype(jnp.int32)
            q_stack = jnp.concatenate(
                [q_scr[par, kv * PAIRS_PER_KV + n, pl.ds(row0, ATTN_BLOCK), :] for n in range(PAIRS_PER_KV)], axis=0)
            for slot in range(2):
                sink = jnp.concatenate(
                    [jnp.full((ATTN_BLOCK, LANES), sinks_ref[2 * (kv * PAIRS_PER_KV + n) + slot], F32)
                     for n in range(PAIRS_PER_KV)], axis=0)
                s = lax.dot_general(q_stack, kz_scr[par, kv, slot, pl.ds(row0, window), :], (((1,), (1,)), ((), ())),
                                    preferred_element_type=F32) + bias_scr[first]
                mx = jnp.maximum(jnp.broadcast_to(jnp.max(s, axis=-1, keepdims=True), (stack, LANES)), sink)
                s_scr[u % 2, slot] = s
                mx_scr[u % 2, slot] = mx
                es_scr[u % 2, slot] = sink - mx

        def values(u):
            kv, i = divmod(u, n_blk)
            row0 = i * ATTN_BLOCK
            outs = []
            for slot in range(2):
                mx = mx_scr[u % 2, slot]
                p = jnp.exp(s_scr[u % 2, slot] - jnp.concatenate([mx, mx], axis=1)).astype(BF16)
                outs.append(jnp.dot(p, vz_scr[par, kv, slot, pl.ds(row0, window), :], preferred_element_type=F32))
            low = lax.broadcasted_iota(jnp.int32, (1, LANES), 1) < HEAD_DIM
            num = jnp.where(low, outs[0], outs[1])
            sums = pltpu.roll(jnp.where(low, outs[1], outs[0]), HEAD_DIM, axis=1)
            den = sums + jnp.exp(jnp.where(low, es_scr[u % 2, 0], es_scr[u % 2, 1]))
            o = num / den
            for n in range(PAIRS_PER_KV):
                o_scr[kv * PAIRS_PER_KV + n, pl.ds(row0, ATTN_BLOCK), :] = o[n * ATTN_BLOCK:(n + 1) * ATTN_BLOCK]

        scores(0)
        for u in range(1, n_units + 1):
            for f in fillers[(u - 1) * len(fillers) // n_units:u * len(fillers) // n_units]:
                f()
            values(u - 1)
            if u < n_units:
                scores(u)

        gate = proj_scr[par, :, _G0:]
        attn = jnp.concatenate([o_scr[r] for r in range(N_Q_HEADS // 2)], axis=1)
        act = (attn * (gate * _sigmoid(gate))).astype(BF16)
        y = xc_ref[...] + jnp.dot(act, wout_ref[...], preferred_element_type=F32)
        if final_norm:
            y = _rms_scale(y, fin_ref[...])
        o_ref[...] = y

    @pl.when(step == 0)
    def _():
        qi = lax.broadcasted_iota(jnp.int32, (stack, window), 0) % ATTN_BLOCK
        kj = lax.broadcasted_iota(jnp.int32, (stack, window), 1)
        dist = qi + ATTN_BLOCK - kj
        band = (dist >= 0) & (dist < ATTN_BLOCK)
        bias_scr[0] = jnp.where(band, 0.0, NEG_INF)
        bias_scr[1] = jnp.where(band & (kj >= ATTN_BLOCK), 0.0, NEG_INF)

    @pl.when(step == 0)
    def _():
        for f in project_tiles(xc_ref, 0):
            f()
        finish_projection(cosc_ref, sinc_ref, 0, None, 0)

    for par in range(2):
        @pl.when(lax.rem(step, 2) == par)
        def _(par=par):
            attend(par, project_tiles(xn_ref, 1 - par))
            finish_projection(cosn_ref, sinn_ref, 1 - par, par, t_nxt)


def _attn_layer(x_bt, norm, win, sinks, w_out, cos, sin, fin, *, final_norm):
    bsz, seq, dm = x_bt.shape
    rows = min(ATTN_ROWS, seq)
    n_t = seq // rows
    n_steps = bsz * n_t
    stack = PAIRS_PER_KV * ATTN_BLOCK

    def nxt(b, t):
        f = jnp.minimum(b * n_t + t + 1, n_steps - 1)
        return f // n_t, f % n_t

    cur_spec = pl.BlockSpec((None, rows, dm), lambda b, t: (b, t, 0))
    nxt_spec = pl.BlockSpec((None, rows, dm), lambda b, t: (*nxt(b, t), 0))
    tab_cur = pl.BlockSpec((rows, LANES), lambda b, t: (t, 0))
    tab_nxt = pl.BlockSpec((rows, LANES), lambda b, t: (nxt(b, t)[1], 0))
    return pl.pallas_call(
        functools.partial(_attn_layer_kernel, rows=rows, n_t=n_t, n_steps=n_steps, final_norm=final_norm),
        out_shape=jax.ShapeDtypeStruct((bsz, seq, dm), F32),
        grid=(bsz, n_t),
        in_specs=[
            pl.BlockSpec(memory_space=pltpu.SMEM),
            cur_spec,
            nxt_spec,
            _const_spec((1, dm)),
            _const_spec((dm, _PROJ_COLS)),
            tab_cur,
            tab_cur,
            tab_nxt,
            tab_nxt,
            _const_spec((BRANCH, dm)),
            _const_spec((1, dm)),
        ],
        out_specs=cur_spec,
        scratch_shapes=[
            pltpu.VMEM((2, rows, _PROJ_COLS), F32),
            pltpu.VMEM((rows, D_MODEL), BF16),
            pltpu.VMEM((2, N_Q_HEADS // 2, rows, LANES), BF16),
            pltpu.VMEM((2, N_KV_HEADS, 2, ATTN_BLOCK + rows, LANES), BF16),
            pltpu.VMEM((2, N_KV_HEADS, 2, ATTN_BLOCK + rows, LANES), BF16),
            pltpu.VMEM((N_Q_HEADS // 2, rows, LANES), F32),
            pltpu.VMEM((2, stack, 2 * ATTN_BLOCK), F32),
            pltpu.VMEM((2, 2, stack, 2 * ATTN_BLOCK), F32),
            pltpu.VMEM((2, 2, stack, LANES), F32),
            pltpu.VMEM((2, 2, stack, LANES), F32),
        ],
        compiler_params=pltpu.CompilerParams(dimension_semantics=("arbitrary", "arbitrary"), vmem_limit_bytes=VMEM_LIMIT),
        name="attn_layer",
    )(sinks.astype(F32), x_bt, x_bt, norm.astype(F32).reshape(1, dm), win, cos, sin, cos, sin, w_out.astype(BF16),
      fin.astype(F32).reshape(1, dm))


def kernel(x, l0_norm, l0_w_in, l0_a_re, l0_a_im, l0_log_step, l0_b_re, l0_b_im, l0_c_re, l0_c_im, l0_d, l0_w_glu, l0_b_glu, l0_w_out, l1_norm, l1_w_in, l1_sinks, l1_w_out, l2_norm, l2_w_in, l2_a_re, l2_a_im, l2_log_step, l2_b_re, l2_b_im, l2_c_re, l2_c_im, l2_d, l2_w_glu, l2_b_glu, l2_w_out, l3_norm, l3_w_in, l3_sinks, l3_w_out, final_norm):
    bsz, seq, _ = x.shape
    assert bsz == BATCH
    cos, sin = _rope_tables(seq)
    mix0, mix2 = _ssm_weights([(l0_a_re, l0_a_im, l0_log_step, l0_b_re, l0_b_im, l0_c_re, l0_c_im),
                               (l2_a_re, l2_a_im, l2_log_step, l2_b_re, l2_b_im, l2_c_re, l2_c_im)])
    win1, win3 = jax.vmap(_attn_weight_layout)(jnp.stack([l1_w_in, l3_w_in]).astype(F32))
    h = x.astype(F32)
    h = _ssm_layer(h, l0_norm, l0_w_in, mix0, l0_d, l0_w_glu, l0_b_glu, l0_w_out)
    h = _attn_layer(h, l1_norm, win1, l1_sinks, l1_w_out, cos, sin, final_norm, final_norm=False)
    h = _ssm_layer(h, l2_norm, l2_w_in, mix2, l2_d, l2_w_glu, l2_b_glu, l2_w_out)
    h = _attn_layer(h, l3_norm, win3, l3_sinks, l3_w_out, cos, sin, final_norm, final_norm=True)
    return h.astype(x.dtype)
```
